```python
import math, functools
import jax, jax.numpy as jnp
from jax import lax
import numpy as np

D_MODEL = 1024
BATCH = 4
SEQ = 8192
DEPTH = 4
DEC_BATCH = 8
DEC_SEQ = 16
PAST_LEN = 2048

CHUNK = 64
QBLK = 128
RW_HEADS = 8
RW_HD = 64
RW_W = RW_HEADS * RW_HD
DECAY_RANK = 64
AAA_RANK = 64
GATE_RANK = 128
RW_COLS = 3 * RW_W + DECAY_RANK + AAA_RANK + GATE_RANK
RW_SPLITS = (RW_W, 2 * RW_W, 3 * RW_W, 3 * RW_W + DECAY_RANK, 3 * RW_W + DECAY_RANK + AAA_RANK)
DF_HEADS = 4
DF_HD = 64
DF_VD = 2 * DF_HD
DF_QK = DF_HEADS * 2 * DF_HD
DF_V = DF_HEADS * DF_VD
DF_COLS = 2 * DF_QK + DF_V
P_IN = RW_COLS + DF_COLS
D_MIX = RW_W + DF_V
D_FF = -(-(8 * D_MODEL) // (3 * 256)) * 256
RMS_EPS = 1e-6
LNX_EPS = 64e-5
SUBLN_EPS = 1e-5

kernel_name = "hymba_rwkv7_diffattn_streaming_step"


def rms_norm(x, g, eps=RMS_EPS):
    xf = x.astype(jnp.float32)
    y = xf * lax.rsqrt(jnp.mean(xf * xf, axis=-1, keepdims=True) + eps)
    return (y * g.astype(jnp.float32)).astype(x.dtype)


def alibi_slopes():
    return 2.0 ** (-8.0 * jnp.arange(1, DF_HEADS + 1, dtype=jnp.float32) / DF_HEADS)


def rwkv7_mix(zr, prev_row, s0, mu, w0, w2, a0, a2, g2, k_k, k_a, r_k, lnx_w, lnx_b):
    f32 = jnp.float32
    B, T, _ = zr.shape
    z_prev = jnp.concatenate([prev_row.astype(zr.dtype), zr[:, :-1]], axis=1)
    zm = zr + (z_prev - zr) * mu
    r, k, v, wd, ad, gd = jnp.split(zm, RW_SPLITS, axis=-1)
    w_log = -jax.nn.softplus(-(w0 + jnp.tanh(wd) @ w2).astype(f32)) - 0.5
    decay = jnp.exp(-jnp.exp(w_log))
    a = jax.nn.sigmoid((a0 + ad @ a2).astype(f32))
    g = jax.nn.sigmoid(gd) @ g2
    hs = lambda t: t.astype(f32).reshape(B, T, RW_HEADS, RW_HD)
    r, k, v, decay, a = hs(r), hs(k), hs(v), hs(decay), hs(a)
    kk = k * k_k.astype(f32).reshape(RW_HEADS, RW_HD)
    kk = kk / jnp.maximum(jnp.linalg.norm(kk, axis=-1, keepdims=True), 1e-12)
    k = k * (1.0 + (a - 1.0) * k_a.astype(f32).reshape(RW_HEADS, RW_HD))

    def step(S, inp):
        r_t, w_t, k_t, v_t, kk_t, a_t = inp
        sa = jnp.einsum('bhvk,bhk->bhv', S, -kk_t)
        S = (S * w_t[:, :, None, :] + sa[..., None] * (kk_t * a_t)[:, :, None, :]
             + v_t[..., None] * k_t[:, :, None, :])
        return S, jnp.einsum('bhvk,bhk->bhv', S, r_t)

    xs = tuple(jnp.moveaxis(t, 1, 0) for t in (r, decay, k, v, kk, a))
    s_fin, y = lax.scan(step, s0.astype(f32), xs)
    y = jnp.moveaxis(y, 0, 1)
    mean = jnp.mean(y, axis=-1, keepdims=True)
    var = jnp.mean(jnp.square(y - mean), axis=-1, keepdims=True)
    y = ((y - mean) * lax.rsqrt(var + LNX_EPS) * lnx_w.astype(f32).reshape(RW_HEADS, RW_HD)
         + lnx_b.astype(f32).reshape(RW_HEADS, RW_HD))
    y = y + jnp.sum(r * k * r_k.astype(f32), axis=-1, keepdims=True) * v
    out = y.reshape(B, T, RW_W).astype(zr.dtype) * g
    return out, zr[:, -1:], s_fin


def diff_attention(q, k, v, q_pos, k_pos, lam, slopes):
    s = jnp.einsum('bqhmd,bkhmd->bmhqk', q, k).astype(jnp.float32) * (DF_HD ** -0.5)
    dist = jnp.abs(q_pos[:, None] - k_pos[None, :]).astype(jnp.float32)
    bias = -slopes[:, None, None] * dist
    mask = (k_pos[None, :] // CHUNK) <= (q_pos[:, None] // CHUNK)
    s = jnp.where(mask, s + bias, -jnp.inf)
    p = jax.nn.softmax(s, axis=-1)
    attn = p[:, 0] - lam * p[:, 1]
    return jnp.einsum('bhqk,bkhe->bqhe', attn.astype(v.dtype), v)


def prompt_attend(q, k, v, lam, slopes):
    B, T = q.shape[0], q.shape[1]
    nb = T // QBLK
    k_pos = jnp.arange(T)
    qb = jnp.moveaxis(q.reshape(B, nb, QBLK, DF_HEADS, 2, DF_HD), 1, 0)
    starts = jnp.arange(nb) * QBLK

    def one_block(args):
        q_blk, st = args
        return diff_attention(q_blk, k, v, st + jnp.arange(QBLK), k_pos, lam, slopes)

    out = lax.map(one_block, (qb, starts))
    return jnp.moveaxis(out, 0, 1).reshape(B, T, DF_HEADS, DF_VD)


def sample_attend(q, k, v, lam, slopes, k_past, v_past):
    B, T = q.shape[0], q.shape[1]
    past = k_past.shape[1]
    k_all = jnp.concatenate([k_past.reshape(B, past, DF_HEADS, 2, DF_HD).astype(k.dtype), k], axis=1)
    v_all = jnp.concatenate([v_past.astype(v.dtype), v], axis=1)
    return diff_attention(q, k_all, v_all, past + jnp.arange(T), jnp.arange(past + T), lam, slopes)


def trunk_layer(x, shift_prev, s0, attend, out_scale, norm1, w_in, mu, w0, w2, a0, a2, g2,
                k_k, k_a, r_k, lnx_w, lnx_b, subln_g, w_out, norm2, wg, wu, wd):
    B, T, _ = x.shape
    h = rms_norm(x, norm1)
    z = h @ w_in
    zr, zd = z[..., :RW_COLS], z[..., RW_COLS:]
    y_rw, last_row, s_fin = rwkv7_mix(zr, shift_prev, s0, mu, w0, w2, a0, a2, g2,
                                      k_k, k_a, r_k, lnx_w, lnx_b)
    q = zd[..., :DF_QK].reshape(B, T, DF_HEADS, 2, DF_HD)
    k = zd[..., DF_QK:2 * DF_QK].reshape(B, T, DF_HEADS, 2, DF_HD)
    v = zd[..., 2 * DF_QK:].reshape(B, T, DF_HEADS, DF_VD)
    o = attend(q, k, v)
    o = rms_norm(o, subln_g, SUBLN_EPS) * out_scale
    mixed = jnp.concatenate([y_rw, o.reshape(B, T, DF_V).astype(x.dtype)], axis=-1)
    x = x + mixed @ w_out
    h2 = rms_norm(x, norm2)
    x = x + (jax.nn.silu(h2 @ wg) * (h2 @ wu)) @ wd
    return x, last_row, s_fin, k.reshape(B, T, DF_HEADS, 2 * DF_HD), v


def setup_inputs(seed: int = 0) -> dict:
    key = jax.random.key(seed)
    ks = iter(jax.random.split(key, 40))
    f32 = jnp.float32
    L = DEPTH

    def nrm(shape, scale):
        return jax.random.normal(next(ks), shape, f32) * scale

    def gain(shape):
        return 1.0 + nrm(shape, 0.02)

    return {
        'x_prompt': nrm((BATCH, SEQ, D_MODEL), 1.0),
        'x_sample': nrm((DEC_BATCH, DEC_SEQ, D_MODEL), 1.0),
        'cache_k': nrm((L, DEC_BATCH, PAST_LEN, DF_HEADS, 2 * DF_HD), 1.0),
        'cache_v': nrm((L, DEC_BATCH, PAST_LEN, DF_HEADS, DF_VD), 1.0),
        'state_wkv': nrm((L, DEC_BATCH, RW_HEADS, RW_HD, RW_HD), 0.5),
        'state_shift': nrm((L, DEC_BATCH, 1, RW_COLS), 1.0),
        'norm1_g': gain((L, D_MODEL)),
        'w_in': nrm((L, D_MODEL, P_IN), D_MODEL ** -0.5),
        'rw_mu': jax.random.uniform(next(ks), (L, RW_COLS), f32),
        'rw_w0': jax.random.uniform(next(ks), (L, RW_W), f32, -4.0, 1.0),
        'rw_w2': nrm((L, DECAY_RANK, RW_W), 0.5 * DECAY_RANK ** -0.5),
        'rw_a0': nrm((L, RW_W), 0.1),
        'rw_a2': nrm((L, AAA_RANK, RW_W), 0.5 * AAA_RANK ** -0.5),
        'rw_g2': nrm((L, GATE_RANK, RW_W), GATE_RANK ** -0.5),
        'rw_k_k': 0.85 + nrm((L, RW_W), 0.02),
        'rw_k_a': gain((L, RW_W)),
        'rw_r_k': nrm((L, RW_HEADS, RW_HD), 0.1),
        'rw_lnx_w': gain((L, RW_W)),
        'rw_lnx_b': nrm((L, RW_W), 0.01),
        'df_lq1': nrm((L, DF_HD), 0.1),
        'df_lk1': nrm((L, DF_HD), 0.1),
        'df_lq2': nrm((L, DF_HD), 0.1),
        'df_lk2': nrm((L, DF_HD), 0.1),
        'df_subln_g': gain((L, DF_VD)),
        'w_out': nrm((L, D_MIX, D_MODEL), D_MIX ** -0.5),
        'norm2_g': gain((L, D_MODEL)),
        'ffn_w_gate': nrm((L, D_MODEL, D_FF), D_MODEL ** -0.5),
        'ffn_w_up': nrm((L, D_MODEL, D_FF), D_MODEL ** -0.5),
        'ffn_w_down': nrm((L, D_FF, D_MODEL), D_FF ** -0.5),
        'final_g': gain((D_MODEL,)),
    }


def reference(x_prompt, x_sample, cache_k, cache_v, state_wkv, state_shift,
              norm1_g, w_in, rw_mu, rw_w0, rw_w2, rw_a0, rw_a2, rw_g2, rw_k_k, rw_k_a, rw_r_k,
              rw_lnx_w, rw_lnx_b, df_lq1, df_lk1, df_lq2, df_lk2, df_subln_g, w_out, norm2_g,
              ffn_w_gate, ffn_w_up, ffn_w_down, final_g):
    f32 = jnp.float32
    slopes = alibi_slopes()
    B = x_prompt.shape[0]
    xp, xs = x_prompt, x_sample
    kp_l, vp_l, sp_l, shp_l = [], [], [], []
    ks_l, vs_l, ss_l, shs_l = [], [], [], []
    for l in range(DEPTH):
        lam_init = 0.8 - 0.6 * math.exp(-0.3 * l)
        lam = (jnp.exp(jnp.sum((df_lq1[l] * df_lk1[l]).astype(f32)))
               - jnp.exp(jnp.sum((df_lq2[l] * df_lk2[l]).astype(f32))) + lam_init)
        lp = (norm1_g[l], w_in[l], rw_mu[l], rw_w0[l], rw_w2[l], rw_a0[l], rw_a2[l], rw_g2[l],
              rw_k_k[l], rw_k_a[l], rw_r_k[l], rw_lnx_w[l], rw_lnx_b[l], df_subln_g[l], w_out[l],
              norm2_g[l], ffn_w_gate[l], ffn_w_up[l], ffn_w_down[l])
        shift0 = jnp.zeros((B, 1, RW_COLS), xp.dtype)
        s0 = jnp.zeros((B, RW_HEADS, RW_HD, RW_HD), f32)
        att_p = functools.partial(prompt_attend, lam=lam, slopes=slopes)
        xp, shp, sp, kp, vp = trunk_layer(xp, shift0, s0, att_p, 1.0 - lam_init, *lp)
        att_s = functools.partial(sample_attend, lam=lam, slopes=slopes,
                                  k_past=cache_k[l], v_past=cache_v[l])
        xs, shs, ss, kss, vss = trunk_layer(xs, state_shift[l], state_wkv[l], att_s, 1.0 - lam_init, *lp)
        kp_l.append(kp); vp_l.append(vp); sp_l.append(sp.astype(state_wkv.dtype)); shp_l.append(shp)
        ks_l.append(kss); vs_l.append(vss); ss_l.append(ss.astype(state_wkv.dtype)); shs_l.append(shs)
    y_prompt = rms_norm(xp, final_g)
    y_sample = rms_norm(xs, final_g)
    return (y_prompt, y_sample,
            jnp.stack(kp_l), jnp.stack(vp_l), jnp.stack(sp_l), jnp.stack(shp_l),
            jnp.stack(ks_l), jnp.stack(vs_l), jnp.stack(ss_l), jnp.stack(shs_l))
```

```python
import functools
import math

import jax
import jax.numpy as jnp
from jax import lax
from jax.experimental import pallas as pl
from jax.experimental.pallas import tpu as pltpu

F32 = jnp.float32
BF16 = jnp.bfloat16

D_MODEL = 1024
CHUNK = 64
RW_HEADS = 8
RW_HD = 64
RW_W = RW_HEADS * RW_HD
DECAY_RANK = 64
AAA_RANK = 64
GATE_RANK = 128
RW_COLS = 3 * RW_W + DECAY_RANK + AAA_RANK + GATE_RANK
DF_HEADS = 4
DF_HD = 64
DF_VD = 2 * DF_HD
DF_QK = DF_HEADS * 2 * DF_HD
DF_V = DF_HEADS * DF_VD
P_IN = RW_COLS + 2 * DF_QK + DF_V
D_FF = 2816
RMS_EPS = 1e-6
LNX_EPS = 64e-5
SUBLN_EPS = 1e-5
QK_SCALE = DF_HD ** -0.5

V7X_LANES = 128
V7X_MXU_COLS = 256
RW_CHUNK = 64
RW_PAIR = 2 * RW_HD
N_PAIRS = RW_HEADS // 2
NEG_BIG = -1e30

INV_PASSES = 3
CHUNK_PASSES = 3
SEG_PASSES = 2

NN = (((1,), (0,)), ((), ()))
NT = (((1,), (1,)), ((), ()))
TN = (((0,), (0,)), ((), ()))


def _vmem_limit(nbytes):
    return pltpu.CompilerParams(vmem_limit_bytes=int(nbytes))


def _row_tile(n, target):
    t = min(n, target)
    while n % t:
        t -= 8
    return t


def _const_spec(shape):
    zeros = (0,) * len(shape)
    return pl.BlockSpec(shape, lambda *_: zeros, pipeline_mode=pl.Buffered(1))


def _split(x, parts):
    out, rem = [], x
    for i in range(parts):
        hi = rem.astype(BF16)
        out.append(hi)
        if i + 1 < parts:
            rem = rem - hi.astype(F32)
    return out


def _mm(a, b, dims=NN, passes=1):
    dot = functools.partial(lax.dot_general, dimension_numbers=dims, preferred_element_type=F32)
    if passes == 1:
        return dot(a.astype(BF16), b.astype(BF16))
    ah, al = _split(a, 2)
    bh, bl = _split(b, 2)
    out = dot(ah, bh) + dot(al, bh)
    if passes >= 3:
        out = out + dot(ah, bl)
    return out


def _rms(x, g, eps):
    return x * lax.rsqrt(jnp.mean(x * x, axis=-1, keepdims=True) + eps) * g


def _inproj_kernel(x_ref, g_ref, w_ref, zr_ref, q_ref, k_ref, v_ref, kb_ref, vb_ref):
    h = _rms(x_ref[...], g_ref[...], RMS_EPS).astype(BF16)
    step = V7X_MXU_COLS

    def proj(c0):
        return jnp.dot(h, w_ref[:, c0:c0 + step], preferred_element_type=F32)

    for c in range(0, RW_COLS, step):
        zr_ref[:, c:c + step] = proj(c)
    for c in range(0, DF_QK, step):
        q_ref[:, c:c + step] = (proj(RW_COLS + c) * QK_SCALE).astype(q_ref.dtype)
        kc = proj(RW_COLS + DF_QK + c)
        k_ref[:, c:c + step] = kc
        kb_ref[:, c:c + step] = kc.astype(kb_ref.dtype)
        vc = proj(RW_COLS + 2 * DF_QK + c)
        v_ref[:, c:c + step] = vc
        vb_ref[:, c:c + step] = vc.astype(vb_ref.dtype)


def _inproj(x2d, g, w_bf):
    n = x2d.shape[0]
    tm = _row_tile(n, 512)
    row = lambda w: pl.BlockSpec((tm, w), lambda i: (i, 0))
    return pl.pallas_call(
        _inproj_kernel,
        grid=(n // tm,),
        in_specs=[row(D_MODEL), _const_spec((1, D_MODEL)), _const_spec((D_MODEL, P_IN))],
        out_specs=[row(RW_COLS), row(DF_QK), row(DF_QK), row(DF_V), row(DF_QK), row(DF_V)],
        out_shape=[
            jax.ShapeDtypeStruct((n, RW_COLS), F32),
            jax.ShapeDtypeStruct((n, DF_QK), BF16),
            jax.ShapeDtypeStruct((n, DF_QK), F32),
            jax.ShapeDtypeStruct((n, DF_V), F32),
            jax.ShapeDtypeStruct((n, DF_QK), BF16),
            jax.ShapeDtypeStruct((n, DF_V), BF16),
        ],
        compiler_params=_vmem_limit(48 << 20),
        name="inproj",
    )(x2d, g.reshape(1, D_MODEL), w_bf)


def _tri_inverse(dab):
    n = dab.shape[0]
    ii = lax.broadcasted_iota(jnp.int32, (n, n), 0)
    jj = lax.broadcasted_iota(jnp.int32, (n, n), 1)
    same16 = (ii >> 4) == (jj >> 4)
    same32 = (ii >> 5) == (jj >> 5)
    mm = functools.partial(_mm, passes=INV_PASSES)
    x = jnp.where(same16, dab, 0.0)
    p = jnp.where(ii == jj, 1.0, 0.0) + x
    for _ in range(3):
        x = mm(x, x)
        p = p + mm(x, p)
    off32 = jnp.where(same32 & jnp.logical_not(same16), dab, 0.0)
    p = p + mm(mm(p, off32), p)
    off64 = jnp.where(same32, 0.0, dab)
    p = p + mm(mm(p, off64), p)
    return p


def _rwkv_chunk(r, lw, k, v, kk, a, s, tril):
    c = RW_CHUNK
    lane = lax.broadcasted_iota(jnp.int32, (c, RW_PAIR), 1)
    head0 = lane < RW_HD

    def stack(x):
        return jnp.concatenate([jnp.where(head0, x, 0.0), jnp.where(head0, 0.0, x)], axis=0)

    lw3 = _split(lw, 3)
    g = sum(jnp.dot(tril, part, preferred_element_type=F32) for part in lw3)
    gc = g[c - 1:c, :]
    eg = jnp.exp(g)
    egp = jnp.exp(g - lw)
    ieg = jnp.exp(-g)
    egc = jnp.exp(gc - g)
    b = kk * a
    lhs = jnp.concatenate([stack(-kk * egp), stack(r * eg)], axis=0)
    bh = b * ieg
    kh = k * ieg
    rhs = jnp.concatenate([bh, bh, kh, kh], axis=0)
    aa = _mm(lhs, rhs, NT, CHUNK_PASSES)

    n = 2 * c
    ii = lax.broadcasted_iota(jnp.int32, (n, n), 0)
    jj = lax.broadcasted_iota(jnp.int32, (n, n), 1)
    same_head = (ii >> 6) == (jj >> 6)
    strict = same_head & (ii > jj)
    incl = same_head & (ii >= jj)
    dab = jnp.where(strict, aa[0:n, 0:n], 0.0)
    dak = jnp.where(strict, aa[0:n, n:2 * n], 0.0)
    drb = jnp.where(incl, aa[n:2 * n, 0:n], 0.0)
    drk = jnp.where(incl, aa[n:2 * n, n:2 * n], 0.0)

    ls = _mm(lhs, s, NT, CHUNK_PASSES)
    vs = stack(v)
    ws = ls[0:n] + _mm(dak, vs, NN, CHUNK_PASSES)
    us = _mm(_tri_inverse(dab), ws, NN, INV_PASSES)
    ys = ls[n:2 * n] + _mm(drb, us, NN, CHUNK_PASSES) + _mm(drk, vs, NN, CHUNK_PASSES)
    y = ys[0:c] + ys[c:n]
    s_new = (s * jnp.exp(gc)
             + _mm(us, stack(b * egc), TN, CHUNK_PASSES)
             + _mm(vs, stack(k * egc), TN, CHUNK_PASSES))
    return y, s_new


def _rwkv_kernel(zr_ref, shift_ref, s0_ref, mu_ref, w0_ref, a0_ref, w2a2_ref, g2_ref, kk_ref, ka_ref,
                 rk_ref, lnw_ref, lnb_ref, seg_ref, y_ref, sfin_ref,
                 s_s, prev_s, r_s, lw_s, k_s, v_s, kkn_s, a_s, y_s, *, tt, t_valid):
    t = pl.program_id(1)

    @pl.when(t == 0)
    def _():
        s_s[...] = s0_ref[0]
        prev_s[0:1, :] = shift_ref[0]

    zr = zr_ref[0]
    row = lax.broadcasted_iota(jnp.int32, (tt, 1), 0)
    zprev = jnp.where(row == 0, prev_s[0:1, :], pltpu.roll(zr, 1, axis=0))
    prev_s[0:1, :] = zr[tt - 1:tt, :]
    zm = zr + (zprev - zr) * mu_ref[...]
    r = zm[:, 0:RW_W]
    k = zm[:, RW_W:2 * RW_W]
    v = zm[:, 2 * RW_W:3 * RW_W]
    xwa = zm[:, 3 * RW_W:3 * RW_W + DECAY_RANK + AAA_RANK]
    gd = zm[:, 3 * RW_W + DECAY_RANK + AAA_RANK:RW_COLS]
    lane = lax.broadcasted_iota(jnp.int32, xwa.shape, 1)
    xwa = jnp.where(lane < DECAY_RANK, jnp.tanh(xwa), xwa)
    lora = jnp.dot(xwa.astype(BF16), w2a2_ref[...], preferred_element_type=F32)
    wneg = -(w0_ref[...] + lora[:, 0:RW_W])
    softplus = jnp.maximum(wneg, 0.0) + jnp.log(1.0 + jnp.exp(-jnp.abs(wneg)))
    lw = -jnp.exp(-softplus - 0.5)
    a = jax.nn.sigmoid(a0_ref[...] + lora[:, RW_W:2 * RW_W])
    gate = jnp.dot(jax.nn.sigmoid(gd).astype(BF16), g2_ref[...], preferred_element_type=F32)

    def segsum(x):
        return sum(jnp.dot(part, seg_ref[...], preferred_element_type=F32) for part in _split(x, SEG_PASSES))

    kkn = k * kk_ref[...]
    kkn = kkn / jnp.maximum(jnp.sqrt(segsum(kkn * kkn)), 1e-12)
    k2 = k * (1.0 + (a - 1.0) * ka_ref[...])
    if t_valid < tt:
        valid = row < t_valid
        lw = jnp.where(valid, lw, 0.0)
        kkn = jnp.where(valid, kkn, 0.0)
        k2 = jnp.where(valid, k2, 0.0)
        v = jnp.where(valid, v, 0.0)
    r_s[...] = r
    lw_s[...] = lw
    k_s[...] = k2
    v_s[...] = v
    kkn_s[...] = kkn
    a_s[...] = a

    c = RW_CHUNK
    ti = lax.broadcasted_iota(jnp.int32, (c, c), 0)
    tj = lax.broadcasted_iota(jnp.int32, (c, c), 1)
    tril = jnp.where(ti >= tj, 1.0, 0.0).astype(BF16)

    def chunk_body(ci, carry):
        rows = pl.ds(pl.multiple_of(ci * c, c), c)
        for p in range(N_PAIRS):
            cols = slice(p * RW_PAIR, (p + 1) * RW_PAIR)
            y, s_new = _rwkv_chunk(r_s[rows, cols], lw_s[rows, cols], k_s[rows, cols], v_s[rows, cols],
                                   kkn_s[rows, cols], a_s[rows, cols], s_s[p], tril)
            y_s[rows, cols] = y
            s_s[p] = s_new
        return carry

    lax.fori_loop(0, tt // c, chunk_body, 0)

    y = y_s[...]
    inv_hd = 1.0 / RW_HD
    d = y - segsum(y) * inv_hd
    yn = d * lax.rsqrt(segsum(d * d) * inv_hd + LNX_EPS) * lnw_ref[...] + lnb_ref[...]
    yn = yn + segsum(r * k2 * rk_ref[...]) * v
    y_ref[0] = (yn * gate).astype(y_ref.dtype)

    @pl.when(t == pl.num_programs(1) - 1)
    def _():
        sfin_ref[0] = s_s[...]


def _rwkv(zr, shift0, s0_pairs, lp, t_valid):
    b, tp, _ = zr.shape
    tt = _row_tile(tp, 256)
    assert t_valid == tp or tp == tt, "padding is only supported inside a single time tile"
    vec = lambda w: _const_spec((1, w))
    scratch_tile = lambda: pltpu.VMEM((tt, RW_W), F32)
    return pl.pallas_call(
        functools.partial(_rwkv_kernel, tt=tt, t_valid=tt if t_valid == tp else t_valid),
        grid=(b, tp // tt),
        in_specs=[
            pl.BlockSpec((1, tt, RW_COLS), lambda i, t: (i, t, 0)),
            pl.BlockSpec((1, 1, RW_COLS), lambda i, t: (i, 0, 0)),
            pl.BlockSpec((1, N_PAIRS, RW_PAIR, RW_PAIR), lambda i, t: (i, 0, 0, 0)),
            vec(RW_COLS), vec(RW_W), vec(RW_W),
            _const_spec((DECAY_RANK + AAA_RANK, 2 * RW_W)), _const_spec((GATE_RANK, RW_W)),
            vec(RW_W), vec(RW_W), vec(RW_W), vec(RW_W), vec(RW_W),
            _const_spec((RW_W, RW_W)),
        ],
        out_specs=[
            pl.BlockSpec((1, tt, RW_W), lambda i, t: (i, t, 0)),
            pl.BlockSpec((1, N_PAIRS, RW_PAIR, RW_PAIR), lambda i, t: (i, 0, 0, 0)),
        ],
        out_shape=[
            jax.ShapeDtypeStruct((b, tp, RW_W), BF16),
            jax.ShapeDtypeStruct((b, N_PAIRS, RW_PAIR, RW_PAIR), F32),
        ],
        scratch_shapes=[
            pltpu.VMEM((N_PAIRS, RW_PAIR, RW_PAIR), F32),
            pltpu.VMEM((8, RW_COLS), F32),
        ] + [scratch_tile() for _ in range(7)],
        compiler_params=pltpu.CompilerParams(
            dimension_semantics=("parallel", "arbitrary"), vmem_limit_bytes=48 << 20),
        name="rwkv",
    )(zr, shift0, s0_pairs, lp["mu"], lp["w0"], lp["a0"], lp["w2a2"], lp["g2"], lp["k_k"], lp["k_a"],
      lp["r_k"], lp["lnx_w"], lp["lnx_b"], lp["seg"])


def _to_pairs(s):
    b = s.shape[0]
    s = s.reshape(b, N_PAIRS, 2, RW_HD, RW_HD)
    eye = jnp.eye(2, dtype=s.dtype)
    return jnp.einsum("bphvk,hg->bphvgk", s, eye).reshape(b, N_PAIRS, RW_PAIR, RW_PAIR)


def _from_pairs(sp):
    b = sp.shape[0]
    sp = sp.reshape(b, N_PAIRS, 2, RW_HD, 2, RW_HD)
    return jnp.stack([sp[:, :, 0, :, 0, :], sp[:, :, 1, :, 1, :]], axis=2).reshape(b, RW_HEADS, RW_HD, RW_HD)


def _stack_maps(q):
    lane = lax.broadcasted_iota(jnp.int32, q.shape, 1)
    zero = jnp.zeros_like(q)
    return jnp.concatenate([jnp.where(lane < DF_HD, q, zero), jnp.where(lane < DF_HD, zero, q)], axis=0)


def _attn_finish(acc, l, lam, out_scale, g, t):
    on = acc / l
    o = on[0:t] - lam * on[t:2 * t]
    return _rms(o, g, SUBLN_EPS) * out_scale


def _attn_prompt_kernel(sc_ref, q_ref, k_ref, v_ref, g_ref, o_ref, m_s, l_s, acc_s, *, tb):
    h = pl.program_id(1)
    i = pl.program_id(2)
    lam, out_scale, slope = sc_ref[0], sc_ref[1], sc_ref[2 + h]
    qs = _stack_maps(q_ref[0])

    def kv(j):
        rows = pl.ds(pl.multiple_of(j * tb, tb), tb)
        return k_ref[0, rows, :], v_ref[0, rows, :]

    ql = lax.broadcasted_iota(jnp.int32, (2 * tb, tb), 0) & (tb - 1)
    kl = lax.broadcasted_iota(jnp.int32, (2 * tb, tb), 1)
    kd, vd = kv(i)
    s = lax.dot_general(qs, kd, NT, preferred_element_type=F32)
    s = s + slope * (ql - jnp.abs(ql - kl)).astype(F32)
    s = jnp.where((kl >> 6) <= (ql >> 6), s, NEG_BIG)
    m = jnp.max(s, axis=-1, keepdims=True)
    p = jnp.exp(s - m)
    m_s[...] = m
    l_s[...] = jnp.sum(p, axis=-1, keepdims=True)
    acc_s[...] = jnp.dot(p.astype(BF16), vd, preferred_element_type=F32)

    kcol = lax.broadcasted_iota(jnp.int32, (1, tb), 1)

    def body(j, carry):
        kj, vj = kv(j)
        s = lax.dot_general(qs, kj, NT, preferred_element_type=F32)
        s = s + slope * (kcol + (j - i) * tb).astype(F32)
        m_old = m_s[...]
        m_new = jnp.maximum(m_old, jnp.max(s, axis=-1, keepdims=True))
        alpha = jnp.exp(m_old - m_new)
        p = jnp.exp(s - m_new)
        m_s[...] = m_new
        l_s[...] = alpha * l_s[...] + jnp.sum(p, axis=-1, keepdims=True)
        acc_s[...] = alpha * acc_s[...] + jnp.dot(p.astype(BF16), vj, preferred_element_type=F32)
        return carry

    lax.fori_loop(0, i, body, 0)
    o_ref[0] = _attn_finish(acc_s[...], l_s[...], lam, out_scale, g_ref[...], tb).astype(o_ref.dtype)


def _attn_prompt(scal, q, k, v, g):
    b, t, _ = q.shape
    tb = _row_tile(t, 256)
    assert tb & (tb - 1) == 0 and tb % CHUNK == 0
    blk = lambda: pl.BlockSpec((1, tb, DF_VD), lambda bi, h, i: (bi, i, h))
    full = lambda: pl.BlockSpec((1, t, DF_VD), lambda bi, h, i: (bi, 0, h))
    return pl.pallas_call(
        functools.partial(_attn_prompt_kernel, tb=tb),
        grid=(b, DF_HEADS, t // tb),
        in_specs=[pl.BlockSpec(memory_space=pltpu.SMEM), blk(), full(), full(), _const_spec((1, DF_VD))],
        out_specs=blk(),
        out_shape=jax.ShapeDtypeStruct((b, t, DF_V), BF16),
        scratch_shapes=[pltpu.VMEM((2 * tb, 1), F32), pltpu.VMEM((2 * tb, 1), F32),
                        pltpu.VMEM((2 * tb, DF_VD), F32)],
        compiler_params=pltpu.CompilerParams(
            dimension_semantics=("parallel", "parallel", "arbitrary"), vmem_limit_bytes=32 << 20),
        name="attn_prompt",
    )(scal, q, k, v, g)


def _attn_sample_kernel(sc_ref, q_ref, kn_ref, vn_ref, kp_ref, vp_ref, g_ref, o_ref, *, t, past):
    h = pl.program_id(1)
    lam, out_scale, slope = sc_ref[0], sc_ref[1], sc_ref[2 + h]
    qs = _stack_maps(q_ref[0])

    def scores(keys, k0):
        nk = keys.shape[0]
        s = lax.dot_general(qs, keys.astype(BF16), NT, preferred_element_type=F32)
        qpos = past + lax.rem(lax.broadcasted_iota(jnp.int32, (2 * t, nk), 0), t)
        kpos = k0 + lax.broadcasted_iota(jnp.int32, (2 * t, nk), 1)
        s = s - slope * jnp.abs(qpos - kpos).astype(F32)
        return jnp.where((kpos >> 6) <= (qpos >> 6), s, NEG_BIG)

    sp = scores(kp_ref[0], 0)
    sn = scores(kn_ref[0], past)
    m = jnp.maximum(jnp.max(sp, axis=-1, keepdims=True), jnp.max(sn, axis=-1, keepdims=True))
    pp = jnp.exp(sp - m)
    pn = jnp.exp(sn - m)
    l = jnp.sum(pp, axis=-1, keepdims=True) + jnp.sum(pn, axis=-1, keepdims=True)
    acc = (jnp.dot(pp.astype(BF16), vp_ref[0].astype(BF16), preferred_element_type=F32)
           + jnp.dot(pn.astype(BF16), vn_ref[0].astype(BF16), preferred_element_type=F32))
    o_ref[0] = _attn_finish(acc, l, lam, out_scale, g_ref[...], t).astype(o_ref.dtype)


def _attn_sample(scal, q, k, v, cache_k, cache_v, layer, g):
    b, t, _ = q.shape
    past = cache_k.shape[2]
    new = lambda: pl.BlockSpec((1, t, DF_VD), lambda bi, h: (bi, 0, h))
    old = lambda: pl.BlockSpec((None, 1, past, DF_VD), lambda bi, h: (layer, bi, 0, h))
    return pl.pallas_call(
        functools.partial(_attn_sample_kernel, t=t, past=past),
        grid=(b, DF_HEADS),
        in_specs=[pl.BlockSpec(memory_space=pltpu.SMEM), new(), new(), new(), old(), old(),
                  _const_spec((1, DF_VD))],
        out_specs=new(),
        out_shape=jax.ShapeDtypeStruct((b, t, DF_V), BF16),
        compiler_params=pltpu.CompilerParams(
            dimension_semantics=("parallel", "parallel"), vmem_limit_bytes=32 << 20),
        name="attn_sample",
    )(scal, q, k, v, cache_k, cache_v, g)


FF_STEP = 256


def _mlp_kernel(x_ref, y_ref, o_ref, wo_ref, g_ref, wg_ref, wu_ref, wd_ref, out_ref, h_s):
    x1 = (x_ref[...]
          + jnp.dot(y_ref[...], wo_ref[0:RW_W, :], preferred_element_type=F32)
          + jnp.dot(o_ref[...], wo_ref[RW_W:RW_W + DF_V, :], preferred_element_type=F32))
    out_ref[...] = x1
    h_s[...] = _rms(x1, g_ref[...], RMS_EPS).astype(BF16)
    for c in range(0, D_FF, FF_STEP):
        h2 = h_s[...]
        gt = jnp.dot(h2, wg_ref[:, c:c + FF_STEP], preferred_element_type=F32)
        up = jnp.dot(h2, wu_ref[:, c:c + FF_STEP], preferred_element_type=F32)
        act = (gt * jax.nn.sigmoid(gt) * up).astype(BF16)
        out_ref[...] += jnp.dot(act, wd_ref[c:c + FF_STEP, :], preferred_element_type=F32)


def _mlp(x2d, y_rw, o_df, wo, g, wg, wu, wd):
    n = x2d.shape[0]
    tm = _row_tile(n, 512)
    row = lambda w: pl.BlockSpec((tm, w), lambda i: (i, 0))
    return pl.pallas_call(
        _mlp_kernel,
        grid=(n // tm,),
        in_specs=[row(D_MODEL), row(RW_W), row(DF_V), _const_spec((RW_W + DF_V, D_MODEL)),
                  _const_spec((1, D_MODEL)), _const_spec((D_MODEL, D_FF)), _const_spec((D_MODEL, D_FF)),
                  _const_spec((D_FF, D_MODEL))],
        out_specs=row(D_MODEL),
        out_shape=jax.ShapeDtypeStruct((n, D_MODEL), F32),
        scratch_shapes=[pltpu.VMEM((tm, D_MODEL), BF16)],
        compiler_params=_vmem_limit(52 << 20),
        name="mlp",
    )(x2d, y_rw, o_df, wo, g.reshape(1, D_MODEL), wg, wu, wd)


def _norm_kernel(x_ref, g_ref, o_ref):
    o_ref[...] = _rms(x_ref[...], g_ref[...], RMS_EPS)


def _final_norm(x2d, g):
    n = x2d.shape[0]
    tm = _row_tile(n, 1024)
    row = pl.BlockSpec((tm, D_MODEL), lambda i: (i, 0))
    return pl.pallas_call(
        _norm_kernel,
        grid=(n // tm,),
        in_specs=[row, _const_spec((1, D_MODEL))],
        out_specs=row,
        out_shape=jax.ShapeDtypeStruct((n, D_MODEL), F32),
        name="final_norm",
    )(x2d, g.reshape(1, D_MODEL))


def _layer(x, shift0, s0_pairs, attend, lp):
    b, t, _ = x.shape
    n = b * t
    zr, q, k, v, kb, vb = _inproj(x.reshape(n, D_MODEL), lp["norm1"], lp["w_in"])
    zr = zr.reshape(b, t, RW_COLS)
    tp = -(-t // RW_CHUNK) * RW_CHUNK
    zr_p = zr if tp == t else jnp.pad(zr, ((0, 0), (0, tp - t), (0, 0)))
    y_rw, s_fin = _rwkv(zr_p, shift0, s0_pairs, lp, t)
    if tp != t:
        y_rw = y_rw[:, :t]
    r3 = lambda u: u.reshape(b, t, u.shape[-1])
    o = attend(r3(q), r3(kb), r3(vb))
    x_new = _mlp(x.reshape(n, D_MODEL), y_rw.reshape(n, RW_W), o.reshape(n, DF_V), lp["w_out"], lp["norm2"],
                 lp["wg"], lp["wu"], lp["wd"])
    return (x_new.reshape(b, t, D_MODEL), zr[:, t - 1:t], s_fin,
            k.reshape(b, t, DF_HEADS, 2 * DF_HD), v.reshape(b, t, DF_HEADS, DF_VD))


@jax.jit
def kernel(x_prompt, x_sample, cache_k, cache_v, state_wkv, state_shift, norm1_g, w_in, rw_mu, rw_w0, rw_w2,
           rw_a0, rw_a2, rw_g2, rw_k_k, rw_k_a, rw_r_k, rw_lnx_w, rw_lnx_b, df_lq1, df_lk1, df_lq2, df_lk2,
           df_subln_g, w_out, norm2_g, ffn_w_gate, ffn_w_up, ffn_w_down, final_g):
    depth = w_in.shape[0]
    bp = x_prompt.shape[0]
    bs, past = cache_k.shape[1], cache_k.shape[2]
    slopes = 2.0 ** (-8.0 * jnp.arange(1, DF_HEADS + 1, dtype=F32) / DF_HEADS)
    head = jnp.arange(RW_W) // RW_HD
    seg = (head[:, None] == head[None, :]).astype(BF16)
    zero_blk = jnp.zeros((DECAY_RANK, RW_W), F32)
    ck = cache_k.reshape(depth, bs, past, DF_QK)
    cv = cache_v.reshape(depth, bs, past, DF_V)
    w_in_b, w_out_b = w_in.astype(BF16), w_out.astype(BF16)
    wg_b, wu_b, wd_b = ffn_w_gate.astype(BF16), ffn_w_up.astype(BF16), ffn_w_down.astype(BF16)

    xp, xs = x_prompt, x_sample
    outs = [[] for _ in range(8)]
    for l in range(depth):
        lam_init = 0.8 - 0.6 * math.exp(-0.3 * l)
        lam = (jnp.exp(jnp.sum((df_lq1[l] * df_lk1[l]).astype(F32)))
               - jnp.exp(jnp.sum((df_lq2[l] * df_lk2[l]).astype(F32))) + lam_init)
        scal = jnp.concatenate([jnp.stack([lam, jnp.asarray(1.0 - lam_init, F32)]), slopes]).astype(F32)
        row = lambda u: u[l].reshape(1, -1)
        lp = dict(
            norm1=norm1_g[l], w_in=w_in_b[l], mu=row(rw_mu), w0=row(rw_w0), a0=row(rw_a0),
            w2a2=jnp.concatenate([jnp.concatenate([rw_w2[l], zero_blk], axis=1),
                                  jnp.concatenate([zero_blk, rw_a2[l]], axis=1)], axis=0).astype(BF16),
            g2=rw_g2[l].astype(BF16), k_k=row(rw_k_k), k_a=row(rw_k_a), r_k=row(rw_r_k),
            lnx_w=row(rw_lnx_w), lnx_b=row(rw_lnx_b), seg=seg, w_out=w_out_b[l], norm2=norm2_g[l],
            wg=wg_b[l], wu=wu_b[l], wd=wd_b[l])
        g_sub = df_subln_g[l].reshape(1, DF_VD)

        att_p = lambda q, k, v: _attn_prompt(scal, q, k, v, g_sub)
        xp, shp, sp, kp, vp = _layer(xp, jnp.zeros((bp, 1, RW_COLS), F32),
                                     jnp.zeros((bp, N_PAIRS, RW_PAIR, RW_PAIR), F32), att_p, lp)
        att_s = lambda q, k, v: _attn_sample(scal, q, k, v, ck, cv, l, g_sub)
        xs, shs, ss, kss, vss = _layer(xs, state_shift[l], _to_pairs(state_wkv[l]), att_s, lp)
        for lst, val in zip(outs, (kp, vp, _from_pairs(sp), shp, kss, vss, _from_pairs(ss), shs)):
            lst.append(val)

    y_prompt = _final_norm(xp.reshape(-1, D_MODEL), final_g).reshape(xp.shape)
    y_sample = _final_norm(xs.reshape(-1, D_MODEL), final_g).reshape(xs.shape)
    return (y_prompt, y_sample) + tuple(jnp.stack(lst) for lst in outs)
```

```python
import functools
import math

import jax
import jax.numpy as jnp
from jax import lax
from jax.experimental import pallas as pl
from jax.experimental.pallas import tpu as pltpu

F32 = jnp.float32
BF16 = jnp.bfloat16

D_MODEL = 1024
CHUNK = 64
RW_HEADS = 8
RW_HD = 64
RW_W = RW_HEADS * RW_HD
DECAY_RANK = 64
AAA_RANK = 64
GATE_RANK = 128
RW_COLS = 3 * RW_W + DECAY_RANK + AAA_RANK + GATE_RANK
DF_HEADS = 4
DF_HD = 64
DF_VD = 2 * DF_HD
DF_QK = DF_HEADS * 2 * DF_HD
DF_V = DF_HEADS * DF_VD
P_IN = RW_COLS + 2 * DF_QK + DF_V
D_FF = 2816
RMS_EPS = 1e-6
LNX_EPS = 64e-5
SUBLN_EPS = 1e-5
QK_SCALE = DF_HD ** -0.5

V7X_LANES = 128
V7X_MXU_COLS = 256
ATTN_BLOCK = 256
ONES_ROWS = 16
ATTN_UNROLL = 2
RW_CHUNK = 64
RW_TILE = 128
RW_PAIR = 2 * RW_HD
N_PAIRS = RW_HEADS // 2
NEG_BIG = -1e30

INV_PASSES = 1
CHUNK_PASSES = 1
STATE_PASSES = 1
SEG_PASSES = 1
CUM_PIECES = 2

NN = (((1,), (0,)), ((), ()))
NT = (((1,), (1,)), ((), ()))
TN = (((0,), (0,)), ((), ()))


def _vmem_limit(nbytes):
    return pltpu.CompilerParams(vmem_limit_bytes=int(nbytes))


def _row_tile(n, target):
    t = min(n, target)
    while n % t:
        t -= 8
    return t


def _const_spec(shape):
    zeros = (0,) * len(shape)
    return pl.BlockSpec(shape, lambda *_: zeros, pipeline_mode=pl.Buffered(1))


def _split(x, parts):
    out, rem = [], x
    for i in range(parts):
        hi = rem.astype(BF16)
        out.append(hi)
        if i + 1 < parts:
            rem = rem - hi.astype(F32)
    return out


def _mm(a, b, dims=NN, passes=1):
    dot = functools.partial(lax.dot_general, dimension_numbers=dims, preferred_element_type=F32)
    if passes == 1:
        return dot(a.astype(BF16), b.astype(BF16))
    ah, al = _split(a, 2)
    if passes == 2:
        bh = b.astype(BF16)
        return dot(ah, bh) + dot(al, bh)
    bh, bl = _split(b, 2)
    return dot(ah, bh) + dot(al, bh) + dot(ah, bl)


def _rms(x, g, eps):
    return x * lax.rsqrt(jnp.mean(x * x, axis=-1, keepdims=True) + eps) * g


def _inproj_kernel(x_ref, g_ref, w_ref, zr_ref, q_ref, k_ref, v_ref, kb_ref, vb_ref, *, vt_block):
    h = _rms(x_ref[...], g_ref[...], RMS_EPS).astype(BF16)
    step = V7X_MXU_COLS

    def proj(c0):
        return jnp.dot(h, w_ref[:, c0:c0 + step], preferred_element_type=F32)

    for c in range(0, RW_COLS, step):
        zr_ref[:, c:c + step] = proj(c)
    for c in range(0, DF_QK, step):
        q_ref[:, c:c + step] = (proj(RW_COLS + c) * QK_SCALE).astype(q_ref.dtype)
        kc = proj(RW_COLS + DF_QK + c)
        k_ref[:, c:c + step] = kc
        kb_ref[:, c:c + step] = kc.astype(kb_ref.dtype)
        vc = proj(RW_COLS + 2 * DF_QK + c)
        v_ref[:, c:c + step] = vc
        if vt_block is None:
            vb_ref[:, c:c + step] = vc.astype(vb_ref.dtype)
        else:
            for rb in range(vb_ref.shape[0]):
                vb_ref[rb, c:c + step, :] = vc[rb * vt_block:(rb + 1) * vt_block, :].T.astype(vb_ref.dtype)


def _inproj(x2d, g, w_bf, vt_block):
    n = x2d.shape[0]
    tm = _row_tile(n, 512)
    row = lambda w: pl.BlockSpec((tm, w), lambda i: (i, 0))
    if vt_block is None:
        vb_spec, vb_shape = row(DF_V), (n, DF_V)
    else:
        assert tm % vt_block == 0
        vb_spec = pl.BlockSpec((tm // vt_block, DF_V, vt_block), lambda i: (i, 0, 0))
        vb_shape = (n // vt_block, DF_V, vt_block)
    return pl.pallas_call(
        functools.partial(_inproj_kernel, vt_block=vt_block),
        grid=(n // tm,),
        in_specs=[row(D_MODEL), _const_spec((1, D_MODEL)), _const_spec((D_MODEL, P_IN))],
        out_specs=[row(RW_COLS), row(DF_QK), row(DF_QK), row(DF_V), row(DF_QK), vb_spec],
        out_shape=[
            jax.ShapeDtypeStruct((n, RW_COLS), F32),
            jax.ShapeDtypeStruct((n, DF_QK), BF16),
            jax.ShapeDtypeStruct((n, DF_QK), F32),
            jax.ShapeDtypeStruct((n, DF_V), F32),
            jax.ShapeDtypeStruct((n, DF_QK), BF16),
            jax.ShapeDtypeStruct(vb_shape, BF16),
        ],
        compiler_params=_vmem_limit(48 << 20),
        name="inproj",
    )(x2d, g.reshape(1, D_MODEL), w_bf)


def _neumann_inverse(dab, passes):
    n = dab[0].shape[0]
    ii = lax.broadcasted_iota(jnp.int32, (n, n), 0)
    jj = lax.broadcasted_iota(jnp.int32, (n, n), 1)
    same16 = (ii >> 4) == (jj >> 4)
    same32 = (ii >> 5) == (jj >> 5)
    eye = jnp.where(ii == jj, 1.0, 0.0)
    mm = functools.partial(_mm, passes=passes)
    x = [jnp.where(same16, d, 0.0) for d in dab]
    p = [eye + xi for xi in x]
    x = [mm(xi, xi) for xi in x]
    for _ in range(2):
        xp = [mm(xi, jnp.concatenate([xi, pi], axis=1)) for xi, pi in zip(x, p)]
        p = [pi + r[:, n:] for pi, r in zip(p, xp)]
        x = [r[:, :n] for r in xp]
    p = [pi + mm(xi, pi) for xi, pi in zip(x, p)]
    for off_mask in (same32 & jnp.logical_not(same16), jnp.logical_not(same32)):
        off = [jnp.where(off_mask, d, 0.0) for d in dab]
        po = [mm(pi, oi) for pi, oi in zip(p, off)]
        p = [pi + mm(qi, pi) for pi, qi in zip(p, po)]
    return p


def _rwkv_kernel(zr_ref, shift_ref, s0_ref, mu_ref, w0_ref, a0_ref, w2a2_ref, g2_ref, kk_ref, ka_ref,
                 rk_ref, lnw_ref, lnb_ref, seg_ref, y_ref, sfin_ref, s_s, prev_s, *, tt, t_valid):
    t = pl.program_id(1)

    @pl.when(t == 0)
    def _():
        s_s[...] = s0_ref[0]
        prev_s[0:1, :] = shift_ref[0]

    zr = zr_ref[0]
    row = lax.broadcasted_iota(jnp.int32, (tt, 1), 0)
    zprev = jnp.where(row == 0, prev_s[0:1, :], pltpu.roll(zr, 1, axis=0))
    prev_s[0:1, :] = zr[tt - 1:tt, :]
    zm = zr + (zprev - zr) * mu_ref[...]
    r = zm[:, 0:RW_W]
    k = zm[:, RW_W:2 * RW_W]
    v = zm[:, 2 * RW_W:3 * RW_W]
    xwa = zm[:, 3 * RW_W:3 * RW_W + DECAY_RANK + AAA_RANK]
    gd = zm[:, 3 * RW_W + DECAY_RANK + AAA_RANK:RW_COLS]
    lane = lax.broadcasted_iota(jnp.int32, xwa.shape, 1)
    xwa = jnp.where(lane < DECAY_RANK, jnp.tanh(xwa), xwa)
    lora = jnp.dot(xwa.astype(BF16), w2a2_ref[...], preferred_element_type=F32)
    wneg = -(w0_ref[...] + lora[:, 0:RW_W])
    softplus = jnp.maximum(wneg, 0.0) + jnp.log(1.0 + jnp.exp(-jnp.abs(wneg)))
    lw = -jnp.exp(-softplus - 0.5)
    a = jax.nn.sigmoid(a0_ref[...] + lora[:, RW_W:2 * RW_W])
    gate = jnp.dot(jax.nn.sigmoid(gd).astype(BF16), g2_ref[...], preferred_element_type=F32)

    def segsum(x):
        return sum(jnp.dot(part, seg_ref[...], preferred_element_type=F32) for part in _split(x, SEG_PASSES))

    kkn = k * kk_ref[...]
    kkn = kkn / jnp.maximum(jnp.sqrt(segsum(kkn * kkn)), 1e-12)
    k2 = k * (1.0 + (a - 1.0) * ka_ref[...])
    if t_valid < tt:
        valid = row < t_valid
        lw = jnp.where(valid, lw, 0.0)
        kkn = jnp.where(valid, kkn, 0.0)
        k2 = jnp.where(valid, k2, 0.0)
        v = jnp.where(valid, v, 0.0)

    c = RW_CHUNK
    ti = lax.broadcasted_iota(jnp.int32, (tt, tt), 0)
    tj = lax.broadcasted_iota(jnp.int32, (tt, tt), 1)
    cum = jnp.where(((ti >> 6) == (tj >> 6)) & (ti >= tj), 1.0, 0.0).astype(BF16)
    g = sum(jnp.dot(cum, part, preferred_element_type=F32) for part in _split(lw, CUM_PIECES))
    gc = jnp.concatenate([jnp.broadcast_to(g[e - 1:e, :], (c, RW_W)) for e in range(c, tt + 1, c)], axis=0)
    ieg = jnp.exp(-g)
    egc = jnp.exp(gc - g)
    b = kkn * a
    at = -kkn * jnp.exp(g - lw)
    rt = r * jnp.exp(g)
    bh, kh = b * ieg, k2 * ieg
    bb, kb = b * egc, k2 * egc
    gam = jnp.exp(gc)

    probs = [(ci, p) for ci in range(tt // c) for p in range(N_PAIRS)]

    def blk(x, ci, p):
        return x[ci * c:(ci + 1) * c, p * RW_PAIR:(p + 1) * RW_PAIR]

    head0 = lax.broadcasted_iota(jnp.int32, (c, RW_PAIR), 1) < RW_HD

    def stack(x):
        return jnp.concatenate([jnp.where(head0, x, 0.0), jnp.where(head0, 0.0, x)], axis=0)

    def twice(x):
        return jnp.concatenate([x, x], axis=0)

    n = 2 * c
    ii = lax.broadcasted_iota(jnp.int32, (n, n), 0)
    jj = lax.broadcasted_iota(jnp.int32, (n, n), 1)
    same_head = (ii >> 6) == (jj >> 6)
    strict = same_head & (ii > jj)
    incl = same_head & (ii >= jj)

    ats = [stack(blk(at, *q)) for q in probs]
    rts = [stack(blk(rt, *q)) for q in probs]
    vs = [stack(blk(v, *q)) for q in probs]
    aa = [_mm(jnp.concatenate([a_, r_], axis=0),
              jnp.concatenate([twice(blk(bh, *q)), twice(blk(kh, *q))], axis=0), NT, CHUNK_PASSES)
          for a_, r_, q in zip(ats, rts, probs)]
    dab = [jnp.where(strict, x[0:n, 0:n], 0.0) for x in aa]
    dak = [jnp.where(strict, x[0:n, n:2 * n], 0.0) for x in aa]
    drbk = [jnp.concatenate([jnp.where(incl, x[n:2 * n, 0:n], 0.0),
                             jnp.where(incl, x[n:2 * n, n:2 * n], 0.0)], axis=1) for x in aa]
    tinv = _neumann_inverse(dab, INV_PASSES)
    dv = [_mm(d, v_, NN, CHUNK_PASSES) for d, v_ in zip(dak, vs)]
    gh = [_mm(t_, jnp.concatenate([a_, d], axis=1), NN, INV_PASSES) for t_, a_, d in zip(tinv, ats, dv)]
    zeros = jnp.zeros((n, n), F32)
    qy = [_mm(d, jnp.concatenate([x, jnp.concatenate([zeros, v_], axis=1)], axis=0), NN, CHUNK_PASSES)
          for d, x, v_ in zip(drbk, gh, vs)]
    qe = [r_ + x[:, 0:n] for r_, x in zip(rts, qy)]
    y0 = [x[:, n:2 * n] for x in qy]
    bbs = [stack(blk(bb, *q)) for q in probs]
    kbs = [stack(blk(kb, *q)) for q in probs]
    mz = [_mm(jnp.concatenate([x, jnp.concatenate([zeros, v_], axis=1)], axis=0),
              jnp.concatenate([b_, k_], axis=0), TN, STATE_PASSES)
          for x, v_, b_, k_ in zip(gh, vs, bbs, kbs)]
    eye = ii == jj
    mx = [jnp.where(eye, blk(gam, *q)[0:1, :], 0.0) + x[0:n] for x, q in zip(mz, probs)]
    zz = [x[n:2 * n] for x in mz]

    state = [s_s[p] for p in range(N_PAIRS)]
    y_rows = []
    for ci in range(tt // c):
        y_pairs = []
        for p in range(N_PAIRS):
            i = ci * N_PAIRS + p
            ys = _mm(qe[i], state[p], NT, STATE_PASSES) + y0[i]
            y_pairs.append(ys[0:c] + ys[c:n])
            state[p] = _mm(state[p], mx[i], NN, STATE_PASSES) + zz[i]
        y_rows.append(jnp.concatenate(y_pairs, axis=1))
    for p in range(N_PAIRS):
        s_s[p] = state[p]
    y = jnp.concatenate(y_rows, axis=0)

    inv_hd = 1.0 / RW_HD
    d = y - segsum(y) * inv_hd
    yn = d * lax.rsqrt(segsum(d * d) * inv_hd + LNX_EPS) * lnw_ref[...] + lnb_ref[...]
    yn = yn + segsum(r * k2 * rk_ref[...]) * v
    y_ref[0] = (yn * gate).astype(y_ref.dtype)

    @pl.when(t == pl.num_programs(1) - 1)
    def _():
        sfin_ref[0] = s_s[...]


def _rwkv(zr, shift0, s0_pairs, lp, t_valid):
    b, tp, _ = zr.shape
    tt = _row_tile(tp, RW_TILE)
    assert t_valid == tp or tp == tt, "padding is only supported inside a single time tile"
    vec = lambda w: _const_spec((1, w))
    return pl.pallas_call(
        functools.partial(_rwkv_kernel, tt=tt, t_valid=tt if t_valid == tp else t_valid),
        grid=(b, tp // tt),
        in_specs=[
            pl.BlockSpec((1, tt, RW_COLS), lambda i, t: (i, t, 0)),
            pl.BlockSpec((1, 1, RW_COLS), lambda i, t: (i, 0, 0)),
            pl.BlockSpec((1, N_PAIRS, RW_PAIR, RW_PAIR), lambda i, t: (i, 0, 0, 0)),
            vec(RW_COLS), vec(RW_W), vec(RW_W),
            _const_spec((DECAY_RANK + AAA_RANK, 2 * RW_W)), _const_spec((GATE_RANK, RW_W)),
            vec(RW_W), vec(RW_W), vec(RW_W), vec(RW_W), vec(RW_W),
            _const_spec((RW_W, RW_W)),
        ],
        out_specs=[
            pl.BlockSpec((1, tt, RW_W), lambda i, t: (i, t, 0)),
            pl.BlockSpec((1, N_PAIRS, RW_PAIR, RW_PAIR), lambda i, t: (i, 0, 0, 0)),
        ],
        out_shape=[
            jax.ShapeDtypeStruct((b, tp, RW_W), BF16),
            jax.ShapeDtypeStruct((b, N_PAIRS, RW_PAIR, RW_PAIR), F32),
        ],
        scratch_shapes=[
            pltpu.VMEM((N_PAIRS, RW_PAIR, RW_PAIR), F32),
            pltpu.VMEM((8, RW_COLS), F32),
        ],
        compiler_params=pltpu.CompilerParams(
            dimension_semantics=("parallel", "arbitrary"), vmem_limit_bytes=48 << 20),
        name="rwkv",
    )(zr, shift0, s0_pairs, lp["mu"], lp["w0"], lp["a0"], lp["w2a2"], lp["g2"], lp["k_k"], lp["k_a"],
      lp["r_k"], lp["lnx_w"], lp["lnx_b"], lp["seg"])


def _to_pairs(s):
    b = s.shape[0]
    s = s.reshape(b, N_PAIRS, 2, RW_HD, RW_HD)
    eye = jnp.eye(2, dtype=s.dtype)
    return jnp.einsum("bphvk,hg->bphvgk", s, eye).reshape(b, N_PAIRS, RW_PAIR, RW_PAIR)


def _from_pairs(sp):
    b = sp.shape[0]
    sp = sp.reshape(b, N_PAIRS, 2, RW_HD, 2, RW_HD)
    return jnp.stack([sp[:, :, 0, :, 0, :], sp[:, :, 1, :, 1, :]], axis=2).reshape(b, RW_HEADS, RW_HD, RW_HD)


def _stack_maps(q):
    lane = lax.broadcasted_iota(jnp.int32, q.shape, 1)
    zero = jnp.zeros_like(q)
    return jnp.concatenate([jnp.where(lane < DF_HD, q, zero), jnp.where(lane < DF_HD, zero, q)], axis=0)


def _attn_prompt_kernel(sc_ref, q_ref, k_ref, kpos_ref, vt_ref, g_ref, dbias_ref, o_ref, m_s, acc_s, s_s, p_s, alpha_s, *, tb):
    h = pl.program_id(1)
    i = pl.program_id(2)
    lam, out_scale, slope = sc_ref[0], sc_ref[1], sc_ref[2 + h]
    qt = q_ref[0].astype(F32).T
    dim = lax.broadcasted_iota(jnp.int32, (DF_VD, tb), 0)
    q_pos = jnp.where(dim == 0, slope * tb, jnp.where(dim == 1, slope, 0.0))
    qs_t = jnp.concatenate(
        [jnp.concatenate([jnp.where(dim < DF_HD, qt, 0.0), jnp.where(dim < DF_HD, 0.0, qt)], axis=1),
         jnp.concatenate([q_pos, q_pos], axis=1)], axis=0).astype(BF16)

    nblk = kpos_ref.shape[0] - 1

    def scores(j, live):
        keys = jnp.concatenate([k_ref[0, jnp.minimum(j, nblk - 1)], kpos_ref[jnp.where(live, j, nblk)]], axis=1)
        return jnp.dot(keys, qs_t, preferred_element_type=F32)

    ones = jnp.ones((ONES_ROWS, tb), BF16)

    def weighted_values(step_idx, p):
        j = jnp.where(step_idx < 0, i, jnp.minimum(step_idx, nblk - 1))
        return jnp.dot(jnp.concatenate([vt_ref[0, j], ones], axis=0), p, preferred_element_type=F32)

    s_s[...] = scores(0, i > 0)
    s = scores(i, True) + dbias_ref[0]
    m = jnp.max(s, axis=0, keepdims=True)
    m_s[...] = m
    p_s[...] = jnp.exp(s - m).astype(BF16)
    alpha_s[...] = jnp.ones(alpha_s.shape, alpha_s.dtype)
    acc_s[...] = jnp.zeros(acc_s.shape, acc_s.dtype)

    def step(j):
        s = s_s[...]
        s_s[...] = scores(j + 1, j + 1 < i)
        acc_s[...] = alpha_s[...] * acc_s[...] + weighted_values(j - 1, p_s[...])
        m_old = m_s[...]
        m_new = jnp.maximum(m_old, jnp.max(s, axis=0, keepdims=True))
        alpha_s[...] = jnp.exp(m_old - m_new)
        p_s[...] = jnp.exp(s - m_new).astype(BF16)
        m_s[...] = m_new

    def body(jj, carry):
        for u in range(ATTN_UNROLL):
            step(jj * ATTN_UNROLL + u)
        return carry

    n_iter = (i + ATTN_UNROLL - 1) // ATTN_UNROLL
    lax.fori_loop(0, n_iter, body, 0)
    acc = alpha_s[...] * acc_s[...] + weighted_values(n_iter * ATTN_UNROLL - 1, p_s[...])
    on = acc[0:DF_VD] / acc[DF_VD:DF_VD + 1]
    ot = on[:, 0:tb] - lam * on[:, tb:2 * tb]
    ot = ot * lax.rsqrt(jnp.mean(ot * ot, axis=0, keepdims=True) + SUBLN_EPS)
    o_ref[0] = (ot.T * g_ref[...] * out_scale).astype(o_ref.dtype)


def _attn_prompt(scal, q, k, vt, g, tb):
    b, t, _ = q.shape
    nb = t // tb
    pos = jnp.arange(t + tb, dtype=jnp.int32).reshape(nb + 1, tb, 1)
    lane = jnp.arange(V7X_LANES, dtype=jnp.int32)
    blk_idx = jnp.where(pos < t, (pos // tb).astype(F32), NEG_BIG)
    kpos = jnp.where(lane == 0, blk_idx, jnp.where(lane == 1, (pos % tb).astype(F32), 0.0)).astype(BF16)
    kl = jnp.arange(tb, dtype=jnp.int32)[:, None]
    ql = jnp.arange(2 * tb, dtype=jnp.int32)[None, :] % tb
    after = jnp.maximum(kl - ql, 0).astype(F32)
    dbias = jnp.where((kl // CHUNK) <= (ql // CHUNK), -2.0 * scal[2:2 + DF_HEADS, None, None] * after, NEG_BIG)
    blk = lambda: pl.BlockSpec((1, tb, DF_VD), lambda bi, h, i: (bi, i, h))
    return pl.pallas_call(
        functools.partial(_attn_prompt_kernel, tb=tb),
        grid=(b, DF_HEADS, nb),
        in_specs=[pl.BlockSpec(memory_space=pltpu.SMEM), blk(),
                  pl.BlockSpec((1, nb, tb, DF_VD), lambda bi, h, i: (bi, 0, 0, h)),
                  _const_spec((nb + 1, tb, V7X_LANES)),
                  pl.BlockSpec((1, nb, DF_VD, tb), lambda bi, h, i: (bi, 0, h, 0)),
                  _const_spec((1, DF_VD)),
                  pl.BlockSpec((1, tb, 2 * tb), lambda bi, h, i: (h, 0, 0))],
        out_specs=blk(),
        out_shape=jax.ShapeDtypeStruct((b, t, DF_V), BF16),
        scratch_shapes=[pltpu.VMEM((1, 2 * tb), F32),
                        pltpu.VMEM((DF_VD + ONES_ROWS, 2 * tb), F32),
                        pltpu.VMEM((tb, 2 * tb), F32),
                        pltpu.VMEM((tb, 2 * tb), BF16),
                        pltpu.VMEM((1, 2 * tb), F32)],
        compiler_params=pltpu.CompilerParams(
            dimension_semantics=("parallel", "parallel", "arbitrary"), vmem_limit_bytes=32 << 20),
        name="attn_prompt",
    )(scal, q, k, kpos, vt, g, dbias)


def _attn_sample_kernel(sc_ref, q_ref, kn_ref, vn_ref, kp_ref, vp_ref, g_ref, o_ref, *, t, past):
    h = pl.program_id(1)
    lam, out_scale, slope = sc_ref[0], sc_ref[1], sc_ref[2 + h]
    qs = _stack_maps(q_ref[0])

    def scores(keys, k0):
        nk = keys.shape[0]
        s = lax.dot_general(qs, keys.astype(BF16), NT, preferred_element_type=F32)
        qpos = past + lax.rem(lax.broadcasted_iota(jnp.int32, (2 * t, nk), 0), t)
        kpos = k0 + lax.broadcasted_iota(jnp.int32, (2 * t, nk), 1)
        s = s - slope * jnp.abs(qpos - kpos).astype(F32)
        return jnp.where((kpos >> 6) <= (qpos >> 6), s, NEG_BIG)

    sp = scores(kp_ref[0], 0)
    sn = scores(kn_ref[0], past)
    m = jnp.maximum(jnp.max(sp, axis=-1, keepdims=True), jnp.max(sn, axis=-1, keepdims=True))
    pp = jnp.exp(sp - m)
    pn = jnp.exp(sn - m)
    l = jnp.sum(pp, axis=-1, keepdims=True) + jnp.sum(pn, axis=-1, keepdims=True)
    acc = (jnp.dot(pp.astype(BF16), vp_ref[0].astype(BF16), preferred_element_type=F32)
           + jnp.dot(pn.astype(BF16), vn_ref[0].astype(BF16), preferred_element_type=F32))
    on = acc / l
    o = on[0:t] - lam * on[t:2 * t]
    o_ref[0] = (_rms(o, g_ref[...], SUBLN_EPS) * out_scale).astype(o_ref.dtype)


def _attn_sample(scal, q, k, v, cache_k, cache_v, layer, g):
    b, t, _ = q.shape
    past = cache_k.shape[2]
    new = lambda: pl.BlockSpec((1, t, DF_VD), lambda bi, h: (bi, 0, h))
    old = lambda: pl.BlockSpec((None, 1, past, DF_VD), lambda bi, h: (layer, bi, 0, h))
    return pl.pallas_call(
        functools.partial(_attn_sample_kernel, t=t, past=past),
        grid=(b, DF_HEADS),
        in_specs=[pl.BlockSpec(memory_space=pltpu.SMEM), new(), new(), new(), old(), old(),
                  _const_spec((1, DF_VD))],
        out_specs=new(),
        out_shape=jax.ShapeDtypeStruct((b, t, DF_V), BF16),
        compiler_params=pltpu.CompilerParams(
            dimension_semantics=("parallel", "parallel"), vmem_limit_bytes=32 << 20),
        name="attn_sample",
    )(scal, q, k, v, cache_k, cache_v, g)


FF_STEP = 256


def _mlp_kernel(x_ref, y_ref, o_ref, wo_ref, g_ref, wg_ref, wu_ref, wd_ref, out_ref, h_s):
    x1 = (x_ref[...]
          + jnp.dot(y_ref[...], wo_ref[0:RW_W, :], preferred_element_type=F32)
          + jnp.dot(o_ref[...], wo_ref[RW_W:RW_W + DF_V, :], preferred_element_type=F32))
    out_ref[...] = x1
    h_s[...] = _rms(x1, g_ref[...], RMS_EPS).astype(BF16)
    for c in range(0, D_FF, FF_STEP):
        h2 = h_s[...]
        gt = jnp.dot(h2, wg_ref[:, c:c + FF_STEP], preferred_element_type=F32)
        up = jnp.dot(h2, wu_ref[:, c:c + FF_STEP], preferred_element_type=F32)
        act = (gt * jax.nn.sigmoid(gt) * up).astype(BF16)
        out_ref[...] += jnp.dot(act, wd_ref[c:c + FF_STEP, :], preferred_element_type=F32)


def _mlp(x2d, y_rw, o_df, wo, g, wg, wu, wd):
    n = x2d.shape[0]
    tm = _row_tile(n, 512)
    row = lambda w: pl.BlockSpec((tm, w), lambda i: (i, 0))
    return pl.pallas_call(
        _mlp_kernel,
        grid=(n // tm,),
        in_specs=[row(D_MODEL), row(RW_W), row(DF_V), _const_spec((RW_W + DF_V, D_MODEL)),
                  _const_spec((1, D_MODEL)), _const_spec((D_MODEL, D_FF)), _const_spec((D_MODEL, D_FF)),
                  _const_spec((D_FF, D_MODEL))],
        out_specs=row(D_MODEL),
        out_shape=jax.ShapeDtypeStruct((n, D_MODEL), F32),
        scratch_shapes=[pltpu.VMEM((tm, D_MODEL), BF16)],
        compiler_params=_vmem_limit(52 << 20),
        name="mlp",
    )(x2d, y_rw, o_df, wo, g.reshape(1, D_MODEL), wg, wu, wd)


def _norm_kernel(x_ref, g_ref, o_ref):
    o_ref[...] = _rms(x_ref[...], g_ref[...], RMS_EPS)


def _final_norm(x2d, g):
    n = x2d.shape[0]
    tm = _row_tile(n, 1024)
    row = pl.BlockSpec((tm, D_MODEL), lambda i: (i, 0))
    return pl.pallas_call(
        _norm_kernel,
        grid=(n // tm,),
        in_specs=[row, _const_spec((1, D_MODEL))],
        out_specs=row,
        out_shape=jax.ShapeDtypeStruct((n, D_MODEL), F32),
        name="final_norm",
    )(x2d, g.reshape(1, D_MODEL))


def _layer(x, shift0, s0_pairs, attend, lp, vt_block):
    b, t, _ = x.shape
    n = b * t
    zr, q, k, v, kb, vb = _inproj(x.reshape(n, D_MODEL), lp["norm1"], lp["w_in"], vt_block)
    zr = zr.reshape(b, t, RW_COLS)
    tp = -(-t // RW_CHUNK) * RW_CHUNK
    zr_p = zr if tp == t else jnp.pad(zr, ((0, 0), (0, tp - t), (0, 0)))
    y_rw, s_fin = _rwkv(zr_p, shift0, s0_pairs, lp, t)
    if tp != t:
        y_rw = y_rw[:, :t]
    o = attend(q.reshape(b, t, DF_QK), kb, vb)
    x_new = _mlp(x.reshape(n, D_MODEL), y_rw.reshape(n, RW_W), o.reshape(n, DF_V), lp["w_out"], lp["norm2"],
                 lp["wg"], lp["wu"], lp["wd"])
    return (x_new.reshape(b, t, D_MODEL), zr[:, t - 1:t], s_fin,
            k.reshape(b, t, DF_HEADS, 2 * DF_HD), v.reshape(b, t, DF_HEADS, DF_VD))


@jax.jit
def kernel(x_prompt, x_sample, cache_k, cache_v, state_wkv, state_shift, norm1_g, w_in, rw_mu, rw_w0, rw_w2,
           rw_a0, rw_a2, rw_g2, rw_k_k, rw_k_a, rw_r_k, rw_lnx_w, rw_lnx_b, df_lq1, df_lk1, df_lq2, df_lk2,
           df_subln_g, w_out, norm2_g, ffn_w_gate, ffn_w_up, ffn_w_down, final_g):
    depth = w_in.shape[0]
    bp, tp = x_prompt.shape[0], x_prompt.shape[1]
    bs, ts = x_sample.shape[0], x_sample.shape[1]
    past = cache_k.shape[2]
    tb = _row_tile(tp, ATTN_BLOCK)
    assert tb & (tb - 1) == 0 and tb % CHUNK == 0
    slopes = 2.0 ** (-8.0 * jnp.arange(1, DF_HEADS + 1, dtype=F32) / DF_HEADS)
    head = jnp.arange(RW_W) // RW_HD
    seg = (head[:, None] == head[None, :]).astype(BF16)
    zero_blk = jnp.zeros((DECAY_RANK, RW_W), F32)
    ck = cache_k.reshape(depth, bs, past, DF_QK)
    cv = cache_v.reshape(depth, bs, past, DF_V)
    w_in_b, w_out_b = w_in.astype(BF16), w_out.astype(BF16)
    wg_b, wu_b, wd_b = ffn_w_gate.astype(BF16), ffn_w_up.astype(BF16), ffn_w_down.astype(BF16)

    xp, xs = x_prompt, x_sample
    outs = [[] for _ in range(8)]
    for l in range(depth):
        lam_init = 0.8 - 0.6 * math.exp(-0.3 * l)
        lam = (jnp.exp(jnp.sum((df_lq1[l] * df_lk1[l]).astype(F32)))
               - jnp.exp(jnp.sum((df_lq2[l] * df_lk2[l]).astype(F32))) + lam_init)
        scal = jnp.concatenate([jnp.stack([lam, jnp.asarray(1.0 - lam_init, F32)]), slopes]).astype(F32)
        row = lambda u: u[l].reshape(1, -1)
        lp = dict(
            norm1=norm1_g[l], w_in=w_in_b[l], mu=row(rw_mu), w0=row(rw_w0), a0=row(rw_a0),
            w2a2=jnp.concatenate([jnp.concatenate([rw_w2[l], zero_blk], axis=1),
                                  jnp.concatenate([zero_blk, rw_a2[l]], axis=1)], axis=0).astype(BF16),
            g2=rw_g2[l].astype(BF16), k_k=row(rw_k_k), k_a=row(rw_k_a), r_k=row(rw_r_k),
            lnx_w=row(rw_lnx_w), lnx_b=row(rw_lnx_b), seg=seg, w_out=w_out_b[l], norm2=norm2_g[l],
            wg=wg_b[l], wu=wu_b[l], wd=wd_b[l])
        g_sub = df_subln_g[l].reshape(1, DF_VD)

        att_p = lambda q, kb, vt: _attn_prompt(scal, q, kb.reshape(bp, tp // tb, tb, DF_QK),
                                               vt.reshape(bp, tp // tb, DF_V, tb), g_sub, tb)
        xp, shp, sp, kp, vp = _layer(xp, jnp.zeros((bp, 1, RW_COLS), F32),
                                     jnp.zeros((bp, N_PAIRS, RW_PAIR, RW_PAIR), F32), att_p, lp, tb)
        att_s = lambda q, kb, vb: _attn_sample(scal, q, kb.reshape(bs, ts, DF_QK), vb.reshape(bs, ts, DF_V),
                                               ck, cv, l, g_sub)
        xs, shs, ss, kss, vss = _layer(xs, state_shift[l], _to_pairs(state_wkv[l]), att_s, lp, None)
        for lst, val in zip(outs, (kp, vp, _from_pairs(sp), shp, kss, vss, _from_pairs(ss), shs)):
            lst.append(val)

    y_prompt = _final_norm(xp.reshape(-1, D_MODEL), final_g).reshape(xp.shape)
    y_sample = _final_norm(xs.reshape(-1, D_MODEL), final_g).reshape(xs.shape)
    return (y_prompt, y_sample) + tuple(jnp.stack(lst) for lst in outs)
```

```python
import functools
import math

import jax
import jax.numpy as jnp
from jax import lax
from jax.experimental import pallas as pl
from jax.experimental.pallas import tpu as pltpu

F32 = jnp.float32
BF16 = jnp.bfloat16

D_MODEL = 1024
CHUNK = 64
RW_HEADS = 8
RW_HD = 64
RW_W = RW_HEADS * RW_HD
DECAY_RANK = 64
AAA_RANK = 64
GATE_RANK = 128
RW_COLS = 3 * RW_W + DECAY_RANK + AAA_RANK + GATE_RANK
DF_HEADS = 4
DF_HD = 64
DF_VD = 2 * DF_HD
DF_QK = DF_HEADS * 2 * DF_HD
DF_V = DF_HEADS * DF_VD
P_IN = RW_COLS + 2 * DF_QK + DF_V
D_FF = 2816
RMS_EPS = 1e-6
LNX_EPS = 64e-5
SUBLN_EPS = 1e-5
QK_SCALE = DF_HD ** -0.5

V7X_LANES = 128
V7X_MXU_COLS = 256
ATTN_BLOCK = 512
POS_SPLIT = 256
ONES_ROWS = 16
ATTN_UNROLL = 1
RW_CHUNK = 64
RW_TILE = 512
RW_SUB = 128
RW_PAIR = 2 * RW_HD
N_PAIRS = RW_HEADS // 2
NEG_BIG = -1e30

INV_PASSES = 1
CHUNK_PASSES = 1
STATE_PASSES = 1
SEG_PASSES = 1
CUM_PIECES = 2

VPU_STAGE, MXU_STAGE = "vpu", "mxu"

NN = (((1,), (0,)), ((), ()))
NT = (((1,), (1,)), ((), ()))
TN = (((0,), (0,)), ((), ()))


def _vmem_limit(nbytes):
    return pltpu.CompilerParams(vmem_limit_bytes=int(nbytes))


def _row_tile(n, target):
    t = min(n, target)
    while n % t:
        t -= 8
    return t


def _const_spec(shape):
    zeros = (0,) * len(shape)
    return pl.BlockSpec(shape, lambda *_: zeros, pipeline_mode=pl.Buffered(1))


def _split(x, parts):
    out, rem = [], x
    for i in range(parts):
        hi = rem.astype(BF16)
        out.append(hi)
        if i + 1 < parts:
            rem = rem - hi.astype(F32)
    return out


def _mm(a, b, dims=NN, passes=1):
    dot = functools.partial(lax.dot_general, dimension_numbers=dims, preferred_element_type=F32)
    if passes == 1:
        return dot(a.astype(BF16), b.astype(BF16))
    ah, al = _split(a, 2)
    if passes == 2:
        bh = b.astype(BF16)
        return dot(ah, bh) + dot(al, bh)
    bh, bl = _split(b, 2)
    return dot(ah, bh) + dot(al, bh) + dot(ah, bl)


def _rms(x, g, eps):
    return x * lax.rsqrt(jnp.mean(x * x, axis=-1, keepdims=True) + eps) * g


def _inproj_kernel(x_ref, g_ref, w_ref, zr_ref, q_ref, k_ref, v_ref, kb_ref, vb_ref, *, vt_block):
    h = _rms(x_ref[...], g_ref[...], RMS_EPS).astype(BF16)
    step = V7X_MXU_COLS

    def proj(c0):
        return jnp.dot(h, w_ref[:, c0:c0 + step], preferred_element_type=F32)

    for c in range(0, RW_COLS, step):
        zr_ref[:, c:c + step] = proj(c)
    for c in range(0, DF_QK, step):
        q_ref[:, c:c + step] = (proj(RW_COLS + c) * QK_SCALE).astype(q_ref.dtype)
        kc = proj(RW_COLS + DF_QK + c)
        k_ref[:, c:c + step] = kc
        kb_ref[:, c:c + step] = kc.astype(kb_ref.dtype)
        vc = proj(RW_COLS + 2 * DF_QK + c)
        v_ref[:, c:c + step] = vc
        if vt_block is None:
            vb_ref[:, c:c + step] = vc.astype(vb_ref.dtype)
        else:
            for rb in range(vb_ref.shape[0]):
                vb_ref[rb, c:c + step, :] = vc[rb * vt_block:(rb + 1) * vt_block, :].T.astype(vb_ref.dtype)


def _inproj(x2d, g, w_bf, vt_block):
    n = x2d.shape[0]
    tm = _row_tile(n, 512)
    row = lambda w: pl.BlockSpec((tm, w), lambda i: (i, 0))
    if vt_block is None:
        vb_spec, vb_shape = row(DF_V), (n, DF_V)
    else:
        assert tm % vt_block == 0
        vb_spec = pl.BlockSpec((tm // vt_block, DF_V, vt_block), lambda i: (i, 0, 0))
        vb_shape = (n // vt_block, DF_V, vt_block)
    return pl.pallas_call(
        functools.partial(_inproj_kernel, vt_block=vt_block),
        grid=(n // tm,),
        in_specs=[row(D_MODEL), _const_spec((1, D_MODEL)), _const_spec((D_MODEL, P_IN))],
        out_specs=[row(RW_COLS), row(DF_QK), row(DF_QK), row(DF_V), row(DF_QK), vb_spec],
        out_shape=[
            jax.ShapeDtypeStruct((n, RW_COLS), F32),
            jax.ShapeDtypeStruct((n, DF_QK), BF16),
            jax.ShapeDtypeStruct((n, DF_QK), F32),
            jax.ShapeDtypeStruct((n, DF_V), F32),
            jax.ShapeDtypeStruct((n, DF_QK), BF16),
            jax.ShapeDtypeStruct(vb_shape, BF16),
        ],
        compiler_params=_vmem_limit(48 << 20),
        name="inproj",
    )(x2d, g.reshape(1, D_MODEL), w_bf)


def _neumann_inverse(dab, passes):
    n = dab[0].shape[0]
    ii = lax.broadcasted_iota(jnp.int32, (n, n), 0)
    jj = lax.broadcasted_iota(jnp.int32, (n, n), 1)
    same16 = (ii >> 4) == (jj >> 4)
    same32 = (ii >> 5) == (jj >> 5)
    eye = jnp.where(ii == jj, 1.0, 0.0)
    mm = functools.partial(_mm, passes=passes)
    x = [jnp.where(same16, d, 0.0) for d in dab]
    p = [eye + xi for xi in x]
    x = [mm(xi, xi) for xi in x]
    yield MXU_STAGE
    for _ in range(2):
        xp = [mm(xi, jnp.concatenate([xi, pi], axis=1)) for xi, pi in zip(x, p)]
        p = [pi + r[:, n:] for pi, r in zip(p, xp)]
        x = [r[:, :n] for r in xp]
        yield MXU_STAGE
    p = [pi + mm(xi, pi) for xi, pi in zip(x, p)]
    yield MXU_STAGE
    for off_mask in (same32 & jnp.logical_not(same16), jnp.logical_not(same32)):
        off = [jnp.where(off_mask, d, 0.0) for d in dab]
        po = [mm(pi, oi) for pi, oi in zip(p, off)]
        yield MXU_STAGE
        p = [pi + mm(qi, pi) for pi, qi in zip(p, po)]
        yield MXU_STAGE
    return p


def _rwkv_rows(zr_ref, mu_ref, w0_ref, a0_ref, w2a2_ref, g2_ref, kk_ref, ka_ref, rk_ref, lnw_ref, lnb_ref, seg_ref,
               y_ref, s_s, prev_s, r0, tt, t_valid, order):
    zr = zr_ref[0, r0:r0 + tt, :]
    row = lax.broadcasted_iota(jnp.int32, (tt, 1), 0)
    before = prev_s[0:1, :] if r0 == 0 else zr_ref[0, r0 - 1:r0, :]
    zprev = jnp.where(row == 0, before, pltpu.roll(zr, 1, axis=0))
    zm = zr + (zprev - zr) * mu_ref[...]
    yield VPU_STAGE
    r = zm[:, 0:RW_W]
    k = zm[:, RW_W:2 * RW_W]
    v = zm[:, 2 * RW_W:3 * RW_W]
    xwa = zm[:, 3 * RW_W:3 * RW_W + DECAY_RANK + AAA_RANK]
    gd = zm[:, 3 * RW_W + DECAY_RANK + AAA_RANK:RW_COLS]
    lane = lax.broadcasted_iota(jnp.int32, xwa.shape, 1)
    xwa = jnp.where(lane < DECAY_RANK, jnp.tanh(xwa), xwa)
    lora = jnp.dot(xwa.astype(BF16), w2a2_ref[...], preferred_element_type=F32)
    yield VPU_STAGE
    lw = -math.exp(-0.5) * jax.nn.sigmoid(w0_ref[...] + lora[:, 0:RW_W])
    a = jax.nn.sigmoid(a0_ref[...] + lora[:, RW_W:2 * RW_W])
    gate = jnp.dot(jax.nn.sigmoid(gd).astype(BF16), g2_ref[...], preferred_element_type=F32)
    yield VPU_STAGE

    def segsum(x):
        return sum(jnp.dot(part, seg_ref[...], preferred_element_type=F32) for part in _split(x, SEG_PASSES))

    kkn = k * kk_ref[...]
    kkn = kkn * lax.rsqrt(jnp.maximum(segsum(kkn * kkn), 1e-24))
    yield VPU_STAGE
    k2 = k * (1.0 + (a - 1.0) * ka_ref[...])
    if t_valid < tt:
        valid = row < t_valid
        lw = jnp.where(valid, lw, 0.0)
        kkn = jnp.where(valid, kkn, 0.0)
        k2 = jnp.where(valid, k2, 0.0)
        v = jnp.where(valid, v, 0.0)

    c = RW_CHUNK
    ti = lax.broadcasted_iota(jnp.int32, (tt, tt), 0)
    tj = lax.broadcasted_iota(jnp.int32, (tt, tt), 1)
    cum = jnp.where(((ti >> 6) == (tj >> 6)) & (ti >= tj), 1.0, 0.0).astype(BF16)
    g = sum(jnp.dot(cum, part, preferred_element_type=F32) for part in _split(lw, CUM_PIECES))
    gc = jnp.concatenate([jnp.broadcast_to(g[e - 1:e, :], (c, RW_W)) for e in range(c, tt + 1, c)], axis=0)
    yield VPU_STAGE
    ieg = jnp.exp(-g)
    egc = jnp.exp(gc - g)
    b = kkn * a
    at = -kkn * jnp.exp(g - lw)
    rt = r * jnp.exp(g)
    yield VPU_STAGE
    bh, kh = b * ieg, k2 * ieg
    bb, kb = b * egc, k2 * egc
    gam = jnp.exp(gc)
    yield VPU_STAGE

    probs = [(ci, p) for ci in range(tt // c) for p in range(N_PAIRS)]

    def blk(x, ci, p):
        return x[ci * c:(ci + 1) * c, p * RW_PAIR:(p + 1) * RW_PAIR]

    head0 = lax.broadcasted_iota(jnp.int32, (c, RW_PAIR), 1) < RW_HD

    def stack(x):
        return jnp.concatenate([jnp.where(head0, x, 0.0), jnp.where(head0, 0.0, x)], axis=0)

    def twice(x):
        return jnp.concatenate([x, x], axis=0)

    n = 2 * c
    ii = lax.broadcasted_iota(jnp.int32, (n, n), 0)
    jj = lax.broadcasted_iota(jnp.int32, (n, n), 1)
    same_head = (ii >> 6) == (jj >> 6)
    strict = same_head & (ii > jj)
    incl = same_head & (ii >= jj)

    ats = [stack(blk(at, *q)) for q in probs]
    rts = [stack(blk(rt, *q)) for q in probs]
    vs = [stack(blk(v, *q)) for q in probs]
    aa = [_mm(jnp.concatenate([a_, r_], axis=0),
              jnp.concatenate([twice(blk(bh, *q)), twice(blk(kh, *q))], axis=0), NT, CHUNK_PASSES)
          for a_, r_, q in zip(ats, rts, probs)]
    yield MXU_STAGE
    dab = [jnp.where(strict, x[0:n, 0:n], 0.0) for x in aa]
    dak = [jnp.where(strict, x[0:n, n:2 * n], 0.0) for x in aa]
    drbk = [jnp.concatenate([jnp.where(incl, x[n:2 * n, 0:n], 0.0),
                             jnp.where(incl, x[n:2 * n, n:2 * n], 0.0)], axis=1) for x in aa]
    dv = [_mm(d, v_, NN, CHUNK_PASSES) for d, v_ in zip(dak, vs)]
    tinv = yield from _neumann_inverse(dab, INV_PASSES)
    gh = [_mm(t_, jnp.concatenate([a_, d], axis=1), NN, INV_PASSES) for t_, a_, d in zip(tinv, ats, dv)]
    yield MXU_STAGE
    zeros = jnp.zeros((n, n), F32)
    qy = [_mm(d, jnp.concatenate([x, jnp.concatenate([zeros, v_], axis=1)], axis=0), NN, CHUNK_PASSES)
          for d, x, v_ in zip(drbk, gh, vs)]
    yield MXU_STAGE
    qe = [r_ + x[:, 0:n] for r_, x in zip(rts, qy)]
    y0 = [x[:, n:2 * n] for x in qy]
    bbs = [stack(blk(bb, *q)) for q in probs]
    kbs = [stack(blk(kb, *q)) for q in probs]
    mz = [_mm(jnp.concatenate([x, jnp.concatenate([zeros, v_], axis=1)], axis=0),
              jnp.concatenate([b_, k_], axis=0), TN, STATE_PASSES)
          for x, v_, b_, k_ in zip(gh, vs, bbs, kbs)]
    yield MXU_STAGE
    eye = ii == jj
    mx = [jnp.where(eye, blk(gam, *q)[0:1, :], 0.0) + x[0:n] for x, q in zip(mz, probs)]
    zz = [x[n:2 * n] for x in mz]

    assert order[0] == r0, "sub-tiles must reach their state update in time order"
    state = [s_s[p] for p in range(N_PAIRS)]
    y_rows = []
    for ci in range(tt // c):
        y_pairs = []
        for p in range(N_PAIRS):
            i = ci * N_PAIRS + p
            ys = _mm(qe[i], state[p], NT, STATE_PASSES) + y0[i]
            y_pairs.append(ys[0:c] + ys[c:n])
            state[p] = _mm(state[p], mx[i], NN, STATE_PASSES) + zz[i]
        y_rows.append(jnp.concatenate(y_pairs, axis=1))
    for p in range(N_PAIRS):
        s_s[p] = state[p]
    order[0] = r0 + tt
    yield MXU_STAGE
    y = jnp.concatenate(y_rows, axis=0)

    inv_hd = 1.0 / RW_HD
    d = y - segsum(y) * inv_hd
    yield VPU_STAGE
    yn = d * lax.rsqrt(segsum(d * d) * inv_hd + LNX_EPS) * lnw_ref[...] + lnb_ref[...]
    yn = yn + segsum(r * k2 * rk_ref[...]) * v
    y_ref[0, r0:r0 + tt, :] = (yn * gate).astype(y_ref.dtype)


def _rwkv_kernel(zr_ref, shift_ref, s0_ref, mu_ref, w0_ref, a0_ref, w2a2_ref, g2_ref, kk_ref, ka_ref,
                 rk_ref, lnw_ref, lnb_ref, seg_ref, y_ref, sfin_ref, s_s, prev_s, *, tt, sub, t_valid):
    t = pl.program_id(1)

    @pl.when(t == 0)
    def _():
        s_s[...] = s0_ref[0]
        prev_s[0:1, :] = shift_ref[0]

    order = [0]
    pending = [_rwkv_rows(zr_ref, mu_ref, w0_ref, a0_ref, w2a2_ref, g2_ref, kk_ref, ka_ref, rk_ref, lnw_ref,
                          lnb_ref, seg_ref, y_ref, s_s, prev_s, r0, sub, t_valid, order)
               for r0 in range(0, tt, sub)]
    active = []
    while pending or active:
        if pending and all(stage == MXU_STAGE for _, stage in active):
            active.append([pending.pop(0), VPU_STAGE])
        for item in list(active):
            try:
                item[1] = next(item[0])
            except StopIteration:
                active.remove(item)
    prev_s[0:1, :] = zr_ref[0, tt - 1:tt, :]

    @pl.when(t == pl.num_programs(1) - 1)
    def _():
        sfin_ref[0] = s_s[...]


def _rwkv(zr, shift0, s0_pairs, lp, t_valid):
    b, tp, _ = zr.shape
    tt = _row_tile(tp, RW_TILE)
    assert t_valid == tp or tp == tt, "padding is only supported inside a single time tile"
    vec = lambda w: _const_spec((1, w))
    return pl.pallas_call(
        functools.partial(_rwkv_kernel, tt=tt, sub=min(tt, RW_SUB), t_valid=tt if t_valid == tp else t_valid),
        grid=(b, tp // tt),
        in_specs=[
            pl.BlockSpec((1, tt, RW_COLS), lambda i, t: (i, t, 0)),
            pl.BlockSpec((1, 1, RW_COLS), lambda i, t: (i, 0, 0)),
            pl.BlockSpec((1, N_PAIRS, RW_PAIR, RW_PAIR), lambda i, t: (i, 0, 0, 0)),
            vec(RW_COLS), vec(RW_W), vec(RW_W),
            _const_spec((DECAY_RANK + AAA_RANK, 2 * RW_W)), _const_spec((GATE_RANK, RW_W)),
            vec(RW_W), vec(RW_W), vec(RW_W), vec(RW_W), vec(RW_W),
            _const_spec((RW_W, RW_W)),
        ],
        out_specs=[
            pl.BlockSpec((1, tt, RW_W), lambda i, t: (i, t, 0)),
            pl.BlockSpec((1, N_PAIRS, RW_PAIR, RW_PAIR), lambda i, t: (i, 0, 0, 0)),
        ],
        out_shape=[
            jax.ShapeDtypeStruct((b, tp, RW_W), BF16),
            jax.ShapeDtypeStruct((b, N_PAIRS, RW_PAIR, RW_PAIR), F32),
        ],
        scratch_shapes=[
            pltpu.VMEM((N_PAIRS, RW_PAIR, RW_PAIR), F32),
            pltpu.VMEM((8, RW_COLS), F32),
        ],
        compiler_params=pltpu.CompilerParams(
            dimension_semantics=("parallel", "arbitrary"), vmem_limit_bytes=48 << 20),
        name="rwkv",
    )(zr, shift0, s0_pairs, lp["mu"], lp["w0"], lp["a0"], lp["w2a2"], lp["g2"], lp["k_k"], lp["k_a"],
      lp["r_k"], lp["lnx_w"], lp["lnx_b"], lp["seg"])


def _to_pairs(s):
    b = s.shape[0]
    s = s.reshape(b, N_PAIRS, 2, RW_HD, RW_HD)
    eye = jnp.eye(2, dtype=s.dtype)
    return jnp.einsum("bphvk,hg->bphvgk", s, eye).reshape(b, N_PAIRS, RW_PAIR, RW_PAIR)


def _from_pairs(sp):
    b = sp.shape[0]
    sp = sp.reshape(b, N_PAIRS, 2, RW_HD, 2, RW_HD)
    return jnp.stack([sp[:, :, 0, :, 0, :], sp[:, :, 1, :, 1, :]], axis=2).reshape(b, RW_HEADS, RW_HD, RW_HD)


def _stack_maps(q):
    lane = lax.broadcasted_iota(jnp.int32, q.shape, 1)
    zero = jnp.zeros_like(q)
    return jnp.concatenate([jnp.where(lane < DF_HD, q, zero), jnp.where(lane < DF_HD, zero, q)], axis=0)


def _attn_prompt_kernel(sc_ref, q_ref, k_ref, kpos_ref, vt_ref, g_ref, dbias_ref, o_ref, m_s, acc_s, s_s, p_s, alpha_s,
                        *, tb):
    h = pl.program_id(1)
    i = pl.program_id(2)
    lam, out_scale, slope = sc_ref[0], sc_ref[1], sc_ref[2 + h]
    qt = q_ref[0].astype(F32).T
    dim = lax.broadcasted_iota(jnp.int32, (DF_VD, tb), 0)
    q_pos = jnp.where(dim == 0, slope * POS_SPLIT, jnp.where(dim == 1, slope, 0.0))
    qs_t = jnp.concatenate(
        [jnp.concatenate([jnp.where(dim < DF_HD, qt, 0.0), jnp.where(dim < DF_HD, 0.0, qt)], axis=1),
         jnp.concatenate([q_pos, q_pos], axis=1)], axis=0).astype(BF16)

    nblk = kpos_ref.shape[0] - 1

    def scores(j, live):
        keys = jnp.concatenate([k_ref[0, jnp.minimum(j, nblk - 1)], kpos_ref[jnp.where(live, j, nblk)]], axis=1)
        return jnp.dot(keys, qs_t, preferred_element_type=F32)

    ones = jnp.ones((ONES_ROWS, tb), BF16)

    def weighted_values(step_idx, p):
        j = jnp.where(step_idx < 0, i, jnp.minimum(step_idx, nblk - 1))
        return jnp.dot(jnp.concatenate([vt_ref[0, j], ones], axis=0), p, preferred_element_type=F32)

    s_s[...] = scores(0, i > 0)
    s = scores(i, True) + dbias_ref[0]
    m = jnp.max(s, axis=0, keepdims=True)
    m_s[...] = m
    p_s[...] = jnp.exp(s - m).astype(BF16)
    alpha_s[...] = jnp.ones(alpha_s.shape, alpha_s.dtype)
    acc_s[...] = jnp.zeros(acc_s.shape, acc_s.dtype)

    def step(j):
        s = s_s[...]
        s_s[...] = scores(j + 1, j + 1 < i)
        acc_s[...] = alpha_s[...] * acc_s[...] + weighted_values(j - 1, p_s[...])
        m_old = m_s[...]
        m_new = jnp.maximum(m_old, jnp.max(s, axis=0, keepdims=True))
        alpha_s[...] = jnp.exp(m_old - m_new)
        p_s[...] = jnp.exp(s - m_new).astype(BF16)
        m_s[...] = m_new

    def body(jj, carry):
        for u in range(ATTN_UNROLL):
            step(jj * ATTN_UNROLL + u)
        return carry

    n_iter = (i + ATTN_UNROLL - 1) // ATTN_UNROLL
    lax.fori_loop(0, n_iter, body, 0)
    acc = alpha_s[...] * acc_s[...] + weighted_values(n_iter * ATTN_UNROLL - 1, p_s[...])
    on = acc[0:DF_VD] / acc[DF_VD:DF_VD + 1]
    ot = on[:, 0:tb] - lam * on[:, tb:2 * tb]
    ot = ot * lax.rsqrt(jnp.mean(ot * ot, axis=0, keepdims=True) + SUBLN_EPS)
    o_ref[0] = (ot.T * g_ref[...] * out_scale).astype(o_ref.dtype)


def _attn_prompt(scal, q, k, vt, g, tb):
    b, t, _ = q.shape
    nb = t // tb
    pos = jnp.arange(t + tb, dtype=jnp.int32).reshape(nb + 1, tb, 1)
    lane = jnp.arange(V7X_LANES, dtype=jnp.int32)
    pos_hi = jnp.where(pos < t, (pos // POS_SPLIT).astype(F32), NEG_BIG)
    kpos = jnp.where(lane == 0, pos_hi, jnp.where(lane == 1, (pos % POS_SPLIT).astype(F32), 0.0)).astype(BF16)
    kl = jnp.arange(tb, dtype=jnp.int32)[:, None]
    ql = jnp.arange(2 * tb, dtype=jnp.int32)[None, :] % tb
    after = jnp.maximum(kl - ql, 0).astype(F32)
    dbias = jnp.where((kl // CHUNK) <= (ql // CHUNK), -2.0 * scal[2:2 + DF_HEADS, None, None] * after, NEG_BIG)
    blk = lambda: pl.BlockSpec((1, tb, DF_VD), lambda bi, h, i: (bi, i, h))
    return pl.pallas_call(
        functools.partial(_attn_prompt_kernel, tb=tb),
        grid=(b, DF_HEADS, nb),
        in_specs=[pl.BlockSpec(memory_space=pltpu.SMEM), blk(),
                  pl.BlockSpec((1, nb, tb, DF_VD), lambda bi, h, i: (bi, 0, 0, h)),
                  _const_spec((nb + 1, tb, V7X_LANES)),
                  pl.BlockSpec((1, nb, DF_VD, tb), lambda bi, h, i: (bi, 0, h, 0)),
                  _const_spec((1, DF_VD)),
                  pl.BlockSpec((1, tb, 2 * tb), lambda bi, h, i: (h, 0, 0))],
        out_specs=blk(),
        out_shape=jax.ShapeDtypeStruct((b, t, DF_V), BF16),
        scratch_shapes=[pltpu.VMEM((1, 2 * tb), F32),
                        pltpu.VMEM((DF_VD + ONES_ROWS, 2 * tb), F32),
                        pltpu.VMEM((tb, 2 * tb), F32),
                        pltpu.VMEM((tb, 2 * tb), BF16),
                        pltpu.VMEM((1, 2 * tb), F32)],
        compiler_params=pltpu.CompilerParams(
            dimension_semantics=("parallel", "parallel", "arbitrary"), vmem_limit_bytes=32 << 20),
        name="attn_prompt",
    )(scal, q, k, kpos, vt, g, dbias)


def _attn_sample_kernel(sc_ref, q_ref, kn_ref, vn_ref, kp_ref, vp_ref, g_ref, o_ref, *, t, past):
    h = pl.program_id(1)
    lam, out_scale, slope = sc_ref[0], sc_ref[1], sc_ref[2 + h]
    qs = _stack_maps(q_ref[0])

    def scores(keys, k0):
        nk = keys.shape[0]
        s = lax.dot_general(qs, keys.astype(BF16), NT, preferred_element_type=F32)
        qpos = past + lax.rem(lax.broadcasted_iota(jnp.int32, (2 * t, nk), 0), t)
        kpos = k0 + lax.broadcasted_iota(jnp.int32, (2 * t, nk), 1)
        s = s - slope * jnp.abs(qpos - kpos).astype(F32)
        return jnp.where((kpos >> 6) <= (qpos >> 6), s, NEG_BIG)

    sp = scores(kp_ref[0], 0)
    sn = scores(kn_ref[0], past)
    m = jnp.maximum(jnp.max(sp, axis=-1, keepdims=True), jnp.max(sn, axis=-1, keepdims=True))
    pp = jnp.exp(sp - m)
    pn = jnp.exp(sn - m)
    l = jnp.sum(pp, axis=-1, keepdims=True) + jnp.sum(pn, axis=-1, keepdims=True)
    acc = (jnp.dot(pp.astype(BF16), vp_ref[0].astype(BF16), preferred_element_type=F32)
           + jnp.dot(pn.astype(BF16), vn_ref[0].astype(BF16), preferred_element_type=F32))
    on = acc / l
    o = on[0:t] - lam * on[t:2 * t]
    o_ref[0] = (_rms(o, g_ref[...], SUBLN_EPS) * out_scale).astype(o_ref.dtype)


def _attn_sample(scal, q, k, v, cache_k, cache_v, layer, g):
    b, t, _ = q.shape
    past = cache_k.shape[2]
    new = lambda: pl.BlockSpec((1, t, DF_VD), lambda bi, h: (bi, 0, h))
    old = lambda: pl.BlockSpec((None, 1, past, DF_VD), lambda bi, h: (layer, bi, 0, h))
    return pl.pallas_call(
        functools.partial(_attn_sample_kernel, t=t, past=past),
        grid=(b, DF_HEADS),
        in_specs=[pl.BlockSpec(memory_space=pltpu.SMEM), new(), new(), new(), old(), old(),
                  _const_spec((1, DF_VD))],
        out_specs=new(),
        out_shape=jax.ShapeDtypeStruct((b, t, DF_V), BF16),
        compiler_params=pltpu.CompilerParams(
            dimension_semantics=("parallel", "parallel"), vmem_limit_bytes=32 << 20),
        name="attn_sample",
    )(scal, q, k, v, cache_k, cache_v, g)


FF_STEP = 256


def _mlp_kernel(x_ref, y_ref, o_ref, wo_ref, g_ref, wg_ref, wu_ref, wd_ref, out_ref, h_s):
    x1 = (x_ref[...]
          + jnp.dot(y_ref[...], wo_ref[0:RW_W, :], preferred_element_type=F32)
          + jnp.dot(o_ref[...], wo_ref[RW_W:RW_W + DF_V, :], preferred_element_type=F32))
    out_ref[...] = x1
    h_s[...] = _rms(x1, g_ref[...], RMS_EPS).astype(BF16)
    for c in range(0, D_FF, FF_STEP):
        h2 = h_s[...]
        gt = jnp.dot(h2, wg_ref[:, c:c + FF_STEP], preferred_element_type=F32)
        up = jnp.dot(h2, wu_ref[:, c:c + FF_STEP], preferred_element_type=F32)
        act = (gt * jax.nn.sigmoid(gt) * up).astype(BF16)
        out_ref[...] += jnp.dot(act, wd_ref[c:c + FF_STEP, :], preferred_element_type=F32)


def _mlp(x2d, y_rw, o_df, wo, g, wg, wu, wd):
    n = x2d.shape[0]
    tm = _row_tile(n, 512)
    row = lambda w: pl.BlockSpec((tm, w), lambda i: (i, 0))
    return pl.pallas_call(
        _mlp_kernel,
        grid=(n // tm,),
        in_specs=[row(D_MODEL), row(RW_W), row(DF_V), _const_spec((RW_W + DF_V, D_MODEL)),
                  _const_spec((1, D_MODEL)), _const_spec((D_MODEL, D_FF)), _const_spec((D_MODEL, D_FF)),
                  _const_spec((D_FF, D_MODEL))],
        out_specs=row(D_MODEL),
        out_shape=jax.ShapeDtypeStruct((n, D_MODEL), F32),
        scratch_shapes=[pltpu.VMEM((tm, D_MODEL), BF16)],
        compiler_params=_vmem_limit(52 << 20),
        name="mlp",
    )(x2d, y_rw, o_df, wo, g.reshape(1, D_MODEL), wg, wu, wd)


def _norm_kernel(x_ref, g_ref, o_ref):
    o_ref[...] = _rms(x_ref[...], g_ref[...], RMS_EPS)


def _final_norm(x2d, g):
    n = x2d.shape[0]
    tm = _row_tile(n, 1024)
    row = pl.BlockSpec((tm, D_MODEL), lambda i: (i, 0))
    return pl.pallas_call(
        _norm_kernel,
        grid=(n // tm,),
        in_specs=[row, _const_spec((1, D_MODEL))],
        out_specs=row,
        out_shape=jax.ShapeDtypeStruct((n, D_MODEL), F32),
        name="final_norm",
    )(x2d, g.reshape(1, D_MODEL))


def _layer(x, shift0, s0_pairs, attend, lp, vt_block):
    b, t, _ = x.shape
    n = b * t
    zr, q, k, v, kb, vb = _inproj(x.reshape(n, D_MODEL), lp["norm1"], lp["w_in"], vt_block)
    zr = zr.reshape(b, t, RW_COLS)
    tp = -(-t // RW_CHUNK) * RW_CHUNK
    zr_p = zr if tp == t else jnp.pad(zr, ((0, 0), (0, tp - t), (0, 0)))
    y_rw, s_fin = _rwkv(zr_p, shift0, s0_pairs, lp, t)
    if tp != t:
        y_rw = y_rw[:, :t]
    o = attend(q.reshape(b, t, DF_QK), kb, vb)
    x_new = _mlp(x.reshape(n, D_MODEL), y_rw.reshape(n, RW_W), o.reshape(n, DF_V), lp["w_out"], lp["norm2"],
                 lp["wg"], lp["wu"], lp["wd"])
    return (x_new.reshape(b, t, D_MODEL), zr[:, t - 1:t], s_fin,
            k.reshape(b, t, DF_HEADS, 2 * DF_HD), v.reshape(b, t, DF_HEADS, DF_VD))


@jax.jit
def kernel(x_prompt, x_sample, cache_k, cache_v, state_wkv, state_shift, norm1_g, w_in, rw_mu, rw_w0, rw_w2,
           rw_a0, rw_a2, rw_g2, rw_k_k, rw_k_a, rw_r_k, rw_lnx_w, rw_lnx_b, df_lq1, df_lk1, df_lq2, df_lk2,
           df_subln_g, w_out, norm2_g, ffn_w_gate, ffn_w_up, ffn_w_down, final_g):
    depth = w_in.shape[0]
    bp, tp = x_prompt.shape[0], x_prompt.shape[1]
    bs, ts = x_sample.shape[0], x_sample.shape[1]
    past = cache_k.shape[2]
    tb = _row_tile(tp, ATTN_BLOCK)
    assert tb & (tb - 1) == 0 and tb % CHUNK == 0
    slopes = 2.0 ** (-8.0 * jnp.arange(1, DF_HEADS + 1, dtype=F32) / DF_HEADS)
    head = jnp.arange(RW_W) // RW_HD
    seg = (head[:, None] == head[None, :]).astype(BF16)
    zero_blk = jnp.zeros((DECAY_RANK, RW_W), F32)
    ck = cache_k.reshape(depth, bs, past, DF_QK)
    cv = cache_v.reshape(depth, bs, past, DF_V)
    w_in_b, w_out_b = w_in.astype(BF16), w_out.astype(BF16)
    wg_b, wu_b, wd_b = ffn_w_gate.astype(BF16), ffn_w_up.astype(BF16), ffn_w_down.astype(BF16)

    xp, xs = x_prompt, x_sample
    outs = [[] for _ in range(8)]
    for l in range(depth):
        lam_init = 0.8 - 0.6 * math.exp(-0.3 * l)
        lam = (jnp.exp(jnp.sum((df_lq1[l] * df_lk1[l]).astype(F32)))
               - jnp.exp(jnp.sum((df_lq2[l] * df_lk2[l]).astype(F32))) + lam_init)
        scal = jnp.concatenate([jnp.stack([lam, jnp.asarray(1.0 - lam_init, F32)]), slopes]).astype(F32)
        row = lambda u: u[l].reshape(1, -1)
        lp = dict(
            norm1=norm1_g[l], w_in=w_in_b[l], mu=row(rw_mu), w0=row(rw_w0), a0=row(rw_a0),
            w2a2=jnp.concatenate([jnp.concatenate([rw_w2[l], zero_blk], axis=1),
                                  jnp.concatenate([zero_blk, rw_a2[l]], axis=1)], axis=0).astype(BF16),
            g2=rw_g2[l].astype(BF16), k_k=row(rw_k_k), k_a=row(rw_k_a), r_k=row(rw_r_k),
            lnx_w=row(rw_lnx_w), lnx_b=row(rw_lnx_b), seg=seg, w_out=w_out_b[l], norm2=norm2_g[l],
            wg=wg_b[l], wu=wu_b[l], wd=wd_b[l])
        g_sub = df_subln_g[l].reshape(1, DF_VD)

        att_p = lambda q, kb, vt: _attn_prompt(scal, q, kb.reshape(bp, tp // tb, tb, DF_QK),
                                               vt.reshape(bp, tp // tb, DF_V, tb), g_sub, tb)
        xp, shp, sp, kp, vp = _layer(xp, jnp.zeros((bp, 1, RW_COLS), F32),
                                     jnp.zeros((bp, N_PAIRS, RW_PAIR, RW_PAIR), F32), att_p, lp, tb)
        att_s = lambda q, kb, vb: _attn_sample(scal, q, kb.reshape(bs, ts, DF_QK), vb.reshape(bs, ts, DF_V),
                                               ck, cv, l, g_sub)
        xs, shs, ss, kss, vss = _layer(xs, state_shift[l], _to_pairs(state_wkv[l]), att_s, lp, None)
        for lst, val in zip(outs, (kp, vp, _from_pairs(sp), shp, kss, vss, _from_pairs(ss), shs)):
            lst.append(val)

    y_prompt = _final_norm(xp.reshape(-1, D_MODEL), final_g).reshape(xp.shape)
    y_sample = _final_norm(xs.reshape(-1, D_MODEL), final_g).reshape(xs.shape)
    return (y_prompt, y_sample) + tuple(jnp.stack(lst) for lst in outs)
```

```python
import functools
import math

import jax
import jax.numpy as jnp
from jax import lax
from jax.experimental import pallas as pl
from jax.experimental.pallas import tpu as pltpu

F32 = jnp.float32
BF16 = jnp.bfloat16

D_MODEL = 1024
CHUNK = 64
RW_HEADS = 8
RW_HD = 64
RW_W = RW_HEADS * RW_HD
DECAY_RANK = 64
AAA_RANK = 64
GATE_RANK = 128
RW_COLS = 3 * RW_W + DECAY_RANK + AAA_RANK + GATE_RANK
DF_HEADS = 4
DF_HD = 64
DF_VD = 2 * DF_HD
DF_QK = DF_HEADS * 2 * DF_HD
DF_V = DF_HEADS * DF_VD
P_IN = RW_COLS + 2 * DF_QK + DF_V
D_FF = 2816
RMS_EPS = 1e-6
LNX_EPS = 64e-5
SUBLN_EPS = 1e-5
QK_SCALE = DF_HD ** -0.5

V7X_LANES = 128
V7X_MXU_COLS = 256
ATTN_BLOCK = 512
POS_SPLIT = 256
ONES_ROWS = 16
RW_CHUNK = 64
RW_TILE = 512
RW_SUB = 128
RW_PAIR = 2 * RW_HD
N_PAIRS = RW_HEADS // 2
NEG_BIG = -1e30

INV_PASSES = 1
CHUNK_PASSES = 1
STATE_PASSES = 1
SEG_PASSES = 1
CUM_PIECES = 2

VPU_STAGE, MXU_STAGE = "vpu", "mxu"

NN = (((1,), (0,)), ((), ()))
NT = (((1,), (1,)), ((), ()))
TN = (((0,), (0,)), ((), ()))


def _vmem_limit(nbytes):
    return pltpu.CompilerParams(vmem_limit_bytes=int(nbytes))


def _row_tile(n, target):
    t = min(n, target)
    while n % t:
        t -= 8
    return t


def _const_spec(shape):
    zeros = (0,) * len(shape)
    return pl.BlockSpec(shape, lambda *_: zeros, pipeline_mode=pl.Buffered(1))


def _split(x, parts):
    out, rem = [], x
    for i in range(parts):
        hi = rem.astype(BF16)
        out.append(hi)
        if i + 1 < parts:
            rem = rem - hi.astype(F32)
    return out


def _mm(a, b, dims=NN, passes=1):
    dot = functools.partial(lax.dot_general, dimension_numbers=dims, preferred_element_type=F32)
    if passes == 1:
        return dot(a.astype(BF16), b.astype(BF16))
    ah, al = _split(a, 2)
    if passes == 2:
        bh = b.astype(BF16)
        return dot(ah, bh) + dot(al, bh)
    bh, bl = _split(b, 2)
    return dot(ah, bh) + dot(al, bh) + dot(ah, bl)


def _rms(x, g, eps):
    return x * lax.rsqrt(jnp.mean(x * x, axis=-1, keepdims=True) + eps) * g


def _inproj_kernel(x_ref, g_ref, w_ref, zr_ref, q_ref, k_ref, v_ref, kb_ref, vb_ref, *, vt_block):
    h = _rms(x_ref[...], g_ref[...], RMS_EPS).astype(BF16)
    step = V7X_MXU_COLS

    def proj(c0):
        return jnp.dot(h, w_ref[:, c0:c0 + step], preferred_element_type=F32)

    for c in range(0, RW_COLS, step):
        zr_ref[:, c:c + step] = proj(c)
    for c in range(0, DF_QK, step):
        q_ref[:, c:c + step] = (proj(RW_COLS + c) * QK_SCALE).astype(q_ref.dtype)
        kc = proj(RW_COLS + DF_QK + c)
        k_ref[:, c:c + step] = kc
        kb_ref[:, c:c + step] = kc.astype(kb_ref.dtype)
        vc = proj(RW_COLS + 2 * DF_QK + c)
        v_ref[:, c:c + step] = vc
        if vt_block is None:
            vb_ref[:, c:c + step] = vc.astype(vb_ref.dtype)
        else:
            for rb in range(vb_ref.shape[0]):
                vb_ref[rb, c:c + step, :] = vc[rb * vt_block:(rb + 1) * vt_block, :].T.astype(vb_ref.dtype)


def _inproj(x2d, g, w_bf, vt_block):
    n = x2d.shape[0]
    tm = _row_tile(n, 512)
    row = lambda w: pl.BlockSpec((tm, w), lambda i: (i, 0))
    if vt_block is None:
        vb_spec, vb_shape = row(DF_V), (n, DF_V)
    else:
        assert tm % vt_block == 0
        vb_spec = pl.BlockSpec((tm // vt_block, DF_V, vt_block), lambda i: (i, 0, 0))
        vb_shape = (n // vt_block, DF_V, vt_block)
    return pl.pallas_call(
        functools.partial(_inproj_kernel, vt_block=vt_block),
        grid=(n // tm,),
        in_specs=[row(D_MODEL), _const_spec((1, D_MODEL)), _const_spec((D_MODEL, P_IN))],
        out_specs=[row(RW_COLS), row(DF_QK), row(DF_QK), row(DF_V), row(DF_QK), vb_spec],
        out_shape=[
            jax.ShapeDtypeStruct((n, RW_COLS), F32),
            jax.ShapeDtypeStruct((n, DF_QK), BF16),
            jax.ShapeDtypeStruct((n, DF_QK), F32),
            jax.ShapeDtypeStruct((n, DF_V), F32),
            jax.ShapeDtypeStruct((n, DF_QK), BF16),
            jax.ShapeDtypeStruct(vb_shape, BF16),
        ],
        compiler_params=_vmem_limit(48 << 20),
        name="inproj",
    )(x2d, g.reshape(1, D_MODEL), w_bf)


def _neumann_inverse(dab, passes):
    n = dab[0].shape[0]
    ii = lax.broadcasted_iota(jnp.int32, (n, n), 0)
    jj = lax.broadcasted_iota(jnp.int32, (n, n), 1)
    same16 = (ii >> 4) == (jj >> 4)
    same32 = (ii >> 5) == (jj >> 5)
    eye = jnp.where(ii == jj, 1.0, 0.0)
    mm = functools.partial(_mm, passes=passes)
    x = [jnp.where(same16, d, 0.0) for d in dab]
    p = [eye + xi for xi in x]
    x = [mm(xi, xi) for xi in x]
    yield MXU_STAGE
    for _ in range(2):
        xp = [mm(xi, jnp.concatenate([xi, pi], axis=1)) for xi, pi in zip(x, p)]
        p = [pi + r[:, n:] for pi, r in zip(p, xp)]
        x = [r[:, :n] for r in xp]
        yield MXU_STAGE
    p = [pi + mm(xi, pi) for xi, pi in zip(x, p)]
    yield MXU_STAGE
    for off_mask in (same32 & jnp.logical_not(same16), jnp.logical_not(same32)):
        off = [jnp.where(off_mask, d, 0.0) for d in dab]
        po = [mm(pi, oi) for pi, oi in zip(p, off)]
        yield MXU_STAGE
        p = [pi + mm(qi, pi) for pi, qi in zip(p, po)]
        yield MXU_STAGE
    return p


def _rwkv_rows(zr_ref, mu_ref, w0_ref, a0_ref, w2a2_ref, g2_ref, kk_ref, ka_ref, rk_ref, lnw_ref, lnb_ref, seg_ref,
               y_ref, s_s, prev_s, r0, tt, t_valid, order):
    zr = zr_ref[0, r0:r0 + tt, :]
    row = lax.broadcasted_iota(jnp.int32, (tt, 1), 0)
    before = prev_s[0:1, :] if r0 == 0 else zr_ref[0, r0 - 1:r0, :]
    zprev = jnp.where(row == 0, before, pltpu.roll(zr, 1, axis=0))
    zm = zr + (zprev - zr) * mu_ref[...]
    yield VPU_STAGE
    r = zm[:, 0:RW_W]
    k = zm[:, RW_W:2 * RW_W]
    v = zm[:, 2 * RW_W:3 * RW_W]
    xwa = zm[:, 3 * RW_W:3 * RW_W + DECAY_RANK + AAA_RANK]
    gd = zm[:, 3 * RW_W + DECAY_RANK + AAA_RANK:RW_COLS]
    lane = lax.broadcasted_iota(jnp.int32, xwa.shape, 1)
    xwa = jnp.where(lane < DECAY_RANK, jnp.tanh(xwa), xwa)
    lora = jnp.dot(xwa.astype(BF16), w2a2_ref[...], preferred_element_type=F32)
    yield VPU_STAGE
    lw = -math.exp(-0.5) * jax.nn.sigmoid(w0_ref[...] + lora[:, 0:RW_W])
    a = jax.nn.sigmoid(a0_ref[...] + lora[:, RW_W:2 * RW_W])
    gate = jnp.dot(jax.nn.sigmoid(gd).astype(BF16), g2_ref[...], preferred_element_type=F32)
    yield VPU_STAGE

    def segsum(x):
        return sum(jnp.dot(part, seg_ref[...], preferred_element_type=F32) for part in _split(x, SEG_PASSES))

    kkn = k * kk_ref[...]
    kkn = kkn * lax.rsqrt(jnp.maximum(segsum(kkn * kkn), 1e-24))
    yield VPU_STAGE
    k2 = k * (1.0 + (a - 1.0) * ka_ref[...])
    if t_valid < tt:
        valid = row < t_valid
        lw = jnp.where(valid, lw, 0.0)
        kkn = jnp.where(valid, kkn, 0.0)
        k2 = jnp.where(valid, k2, 0.0)
        v = jnp.where(valid, v, 0.0)

    c = RW_CHUNK
    ti = lax.broadcasted_iota(jnp.int32, (tt, tt), 0)
    tj = lax.broadcasted_iota(jnp.int32, (tt, tt), 1)
    cum = jnp.where(((ti >> 6) == (tj >> 6)) & (ti >= tj), 1.0, 0.0).astype(BF16)
    g = sum(jnp.dot(cum, part, preferred_element_type=F32) for part in _split(lw, CUM_PIECES))
    gc = jnp.concatenate([jnp.broadcast_to(g[e - 1:e, :], (c, RW_W)) for e in range(c, tt + 1, c)], axis=0)
    yield VPU_STAGE
    ieg = jnp.exp(-g)
    egc = jnp.exp(gc - g)
    b = kkn * a
    at = -kkn * jnp.exp(g - lw)
    rt = r * jnp.exp(g)
    yield VPU_STAGE
    bh, kh = b * ieg, k2 * ieg
    bb, kb = b * egc, k2 * egc
    gam = jnp.exp(gc)
    yield VPU_STAGE

    probs = [(ci, p) for ci in range(tt // c) for p in range(N_PAIRS)]

    def blk(x, ci, p):
        return x[ci * c:(ci + 1) * c, p * RW_PAIR:(p + 1) * RW_PAIR]

    head0 = lax.broadcasted_iota(jnp.int32, (c, RW_PAIR), 1) < RW_HD

    def stack(x):
        return jnp.concatenate([jnp.where(head0, x, 0.0), jnp.where(head0, 0.0, x)], axis=0)

    def twice(x):
        return jnp.concatenate([x, x], axis=0)

    n = 2 * c
    ii = lax.broadcasted_iota(jnp.int32, (n, n), 0)
    jj = lax.broadcasted_iota(jnp.int32, (n, n), 1)
    same_head = (ii >> 6) == (jj >> 6)
    strict = same_head & (ii > jj)
    incl = same_head & (ii >= jj)

    ats = [stack(blk(at, *q)) for q in probs]
    rts = [stack(blk(rt, *q)) for q in probs]
    vs = [stack(blk(v, *q)) for q in probs]
    aa = [_mm(jnp.concatenate([a_, r_], axis=0),
              jnp.concatenate([twice(blk(bh, *q)), twice(blk(kh, *q))], axis=0), NT, CHUNK_PASSES)
          for a_, r_, q in zip(ats, rts, probs)]
    yield MXU_STAGE
    dab = [jnp.where(strict, x[0:n, 0:n], 0.0) for x in aa]
    dak = [jnp.where(strict, x[0:n, n:2 * n], 0.0) for x in aa]
    drbk = [jnp.concatenate([jnp.where(incl, x[n:2 * n, 0:n], 0.0),
                             jnp.where(incl, x[n:2 * n, n:2 * n], 0.0)], axis=1) for x in aa]
    dv = [_mm(d, v_, NN, CHUNK_PASSES) for d, v_ in zip(dak, vs)]
    tinv = yield from _neumann_inverse(dab, INV_PASSES)
    gh = [_mm(t_, jnp.concatenate([a_, d], axis=1), NN, INV_PASSES) for t_, a_, d in zip(tinv, ats, dv)]
    yield MXU_STAGE
    zeros = jnp.zeros((n, n), F32)
    qy = [_mm(d, jnp.concatenate([x, jnp.concatenate([zeros, v_], axis=1)], axis=0), NN, CHUNK_PASSES)
          for d, x, v_ in zip(drbk, gh, vs)]
    yield MXU_STAGE
    qe = [r_ + x[:, 0:n] for r_, x in zip(rts, qy)]
    y0 = [x[:, n:2 * n] for x in qy]
    bbs = [stack(blk(bb, *q)) for q in probs]
    kbs = [stack(blk(kb, *q)) for q in probs]
    mz = [_mm(jnp.concatenate([x, jnp.concatenate([zeros, v_], axis=1)], axis=0),
              jnp.concatenate([b_, k_], axis=0), TN, STATE_PASSES)
          for x, v_, b_, k_ in zip(gh, vs, bbs, kbs)]
    yield MXU_STAGE
    eye = ii == jj
    mx = [jnp.where(eye, blk(gam, *q)[0:1, :], 0.0) + x[0:n] for x, q in zip(mz, probs)]
    zz = [x[n:2 * n] for x in mz]

    assert order[0] == r0, "sub-tiles must reach their state update in time order"
    state = [s_s[p] for p in range(N_PAIRS)]
    y_rows = []
    for ci in range(tt // c):
        y_pairs = []
        for p in range(N_PAIRS):
            i = ci * N_PAIRS + p
            ys = _mm(qe[i], state[p], NT, STATE_PASSES) + y0[i]
            y_pairs.append(ys[0:c] + ys[c:n])
            state[p] = _mm(state[p], mx[i], NN, STATE_PASSES) + zz[i]
        y_rows.append(jnp.concatenate(y_pairs, axis=1))
    for p in range(N_PAIRS):
        s_s[p] = state[p]
    order[0] = r0 + tt
    yield MXU_STAGE
    y = jnp.concatenate(y_rows, axis=0)

    inv_hd = 1.0 / RW_HD
    d = y - segsum(y) * inv_hd
    yield VPU_STAGE
    yn = d * lax.rsqrt(segsum(d * d) * inv_hd + LNX_EPS) * lnw_ref[...] + lnb_ref[...]
    yn = yn + segsum(r * k2 * rk_ref[...]) * v
    y_ref[0, r0:r0 + tt, :] = (yn * gate).astype(y_ref.dtype)


def _rwkv_kernel(zr_ref, shift_ref, s0_ref, mu_ref, w0_ref, a0_ref, w2a2_ref, g2_ref, kk_ref, ka_ref,
                 rk_ref, lnw_ref, lnb_ref, seg_ref, y_ref, sfin_ref, s_s, prev_s, *, tt, sub, t_valid):
    t = pl.program_id(1)

    @pl.when(t == 0)
    def _():
        s_s[...] = s0_ref[0]
        prev_s[0:1, :] = shift_ref[0]

    order = [0]
    pending = [_rwkv_rows(zr_ref, mu_ref, w0_ref, a0_ref, w2a2_ref, g2_ref, kk_ref, ka_ref, rk_ref, lnw_ref,
                          lnb_ref, seg_ref, y_ref, s_s, prev_s, r0, sub, t_valid, order)
               for r0 in range(0, tt, sub)]
    active = []
    while pending or active:
        if pending and all(stage == MXU_STAGE for _, stage in active):
            active.append([pending.pop(0), VPU_STAGE])
        for item in list(active):
            try:
                item[1] = next(item[0])
            except StopIteration:
                active.remove(item)
    prev_s[0:1, :] = zr_ref[0, tt - 1:tt, :]

    @pl.when(t == pl.num_programs(1) - 1)
    def _():
        sfin_ref[0] = s_s[...]


def _rwkv(zr, shift0, s0_pairs, lp, t_valid):
    b, tp, _ = zr.shape
    tt = _row_tile(tp, RW_TILE)
    assert t_valid == tp or tp == tt, "padding is only supported inside a single time tile"
    vec = lambda w: _const_spec((1, w))
    return pl.pallas_call(
        functools.partial(_rwkv_kernel, tt=tt, sub=min(tt, RW_SUB), t_valid=tt if t_valid == tp else t_valid),
        grid=(b, tp // tt),
        in_specs=[
            pl.BlockSpec((1, tt, RW_COLS), lambda i, t: (i, t, 0)),
            pl.BlockSpec((1, 1, RW_COLS), lambda i, t: (i, 0, 0)),
            pl.BlockSpec((1, N_PAIRS, RW_PAIR, RW_PAIR), lambda i, t: (i, 0, 0, 0)),
            vec(RW_COLS), vec(RW_W), vec(RW_W),
            _const_spec((DECAY_RANK + AAA_RANK, 2 * RW_W)), _const_spec((GATE_RANK, RW_W)),
            vec(RW_W), vec(RW_W), vec(RW_W), vec(RW_W), vec(RW_W),
            _const_spec((RW_W, RW_W)),
        ],
        out_specs=[
            pl.BlockSpec((1, tt, RW_W), lambda i, t: (i, t, 0)),
            pl.BlockSpec((1, N_PAIRS, RW_PAIR, RW_PAIR), lambda i, t: (i, 0, 0, 0)),
        ],
        out_shape=[
            jax.ShapeDtypeStruct((b, tp, RW_W), BF16),
            jax.ShapeDtypeStruct((b, N_PAIRS, RW_PAIR, RW_PAIR), F32),
        ],
        scratch_shapes=[
            pltpu.VMEM((N_PAIRS, RW_PAIR, RW_PAIR), F32),
            pltpu.VMEM((8, RW_COLS), F32),
        ],
        compiler_params=pltpu.CompilerParams(
            dimension_semantics=("parallel", "arbitrary"), vmem_limit_bytes=48 << 20),
        name="rwkv",
    )(zr, shift0, s0_pairs, lp["mu"], lp["w0"], lp["a0"], lp["w2a2"], lp["g2"], lp["k_k"], lp["k_a"],
      lp["r_k"], lp["lnx_w"], lp["lnx_b"], lp["seg"])


def _to_pairs(s):
    b = s.shape[0]
    s = s.reshape(b, N_PAIRS, 2, RW_HD, RW_HD)
    eye = jnp.eye(2, dtype=s.dtype)
    return jnp.einsum("bphvk,hg->bphvgk", s, eye).reshape(b, N_PAIRS, RW_PAIR, RW_PAIR)


def _from_pairs(sp):
    b = sp.shape[0]
    sp = sp.reshape(b, N_PAIRS, 2, RW_HD, 2, RW_HD)
    return jnp.stack([sp[:, :, 0, :, 0, :], sp[:, :, 1, :, 1, :]], axis=2).reshape(b, RW_HEADS, RW_HD, RW_HD)


def _stack_maps(q):
    lane = lax.broadcasted_iota(jnp.int32, q.shape, 1)
    zero = jnp.zeros_like(q)
    return jnp.concatenate([jnp.where(lane < DF_HD, q, zero), jnp.where(lane < DF_HD, zero, q)], axis=0)


def _attn_prompt_kernel(sc_ref, q_ref, k_ref, kpos_ref, vt_ref, g_ref, dbias_ref, o_ref, m_s, acc_s, s_s, smax_s,
                        *, tb):
    h = pl.program_id(1)
    i = pl.program_id(2)
    lam, out_scale, slope = sc_ref[0], sc_ref[1], sc_ref[2 + h]
    qt = q_ref[0].astype(F32).T
    dim = lax.broadcasted_iota(jnp.int32, (DF_VD, tb), 0)
    q_pos = jnp.where(dim == 0, slope * POS_SPLIT, jnp.where(dim == 1, slope, 0.0))
    qs_t = jnp.concatenate(
        [jnp.concatenate([jnp.where(dim < DF_HD, qt, 0.0), jnp.where(dim < DF_HD, 0.0, qt)], axis=1),
         jnp.concatenate([q_pos, q_pos], axis=1)], axis=0).astype(BF16)

    def scores(j):
        keys = jnp.concatenate([k_ref[0, j], kpos_ref[j]], axis=1)
        return jnp.dot(keys, qs_t, preferred_element_type=F32)

    ones = jnp.ones((ONES_ROWS, tb), BF16)

    def weighted_values(j, p):
        return jnp.dot(jnp.concatenate([vt_ref[0, j], ones], axis=0), p, preferred_element_type=F32)

    def stash_scores(j):
        sc = scores(j)
        s_s[...] = sc
        smax_s[...] = jnp.max(sc, axis=0, keepdims=True)

    stash_scores(0)
    s = scores(i) + dbias_ref[0]
    m = jnp.max(s, axis=0, keepdims=True)
    m_s[...] = m
    acc_s[...] = weighted_values(i, jnp.exp(s - m).astype(BF16))

    def step(j, stash_next):
        s = s_s[...]
        m_old = m_s[...]
        m_new = jnp.maximum(m_old, smax_s[...])
        if stash_next:
            stash_scores(j + 1)
        p = jnp.exp(s - m_new).astype(BF16)
        acc_s[...] = jnp.exp(m_old - m_new) * acc_s[...] + weighted_values(j, p)
        m_s[...] = m_new

    n_lead = jnp.maximum(i - 1, 0)

    def two_steps(jj, carry):
        step(2 * jj, True)
        step(2 * jj + 1, True)
        return carry

    lax.fori_loop(0, n_lead // 2, two_steps, 0)

    @pl.when(n_lead % 2 == 1)
    def _():
        step(n_lead - 1, True)

    @pl.when(i > 0)
    def _():
        step(i - 1, False)

    acc = acc_s[...]
    on = acc[0:DF_VD] / acc[DF_VD:DF_VD + 1]
    ot = on[:, 0:tb] - lam * on[:, tb:2 * tb]
    ot = ot * lax.rsqrt(jnp.mean(ot * ot, axis=0, keepdims=True) + SUBLN_EPS)
    o_ref[0] = (ot.T * g_ref[...] * out_scale).astype(o_ref.dtype)


def _attn_prompt(scal, q, k, vt, g, tb):
    b, t, _ = q.shape
    nb = t // tb
    pos = jnp.arange(t, dtype=jnp.int32).reshape(nb, tb, 1)
    lane = jnp.arange(V7X_LANES, dtype=jnp.int32)
    kpos = jnp.where(lane == 0, pos // POS_SPLIT, jnp.where(lane == 1, pos % POS_SPLIT, 0)).astype(BF16)
    kl = jnp.arange(tb, dtype=jnp.int32)[:, None]
    ql = jnp.arange(2 * tb, dtype=jnp.int32)[None, :] % tb
    after = jnp.maximum(kl - ql, 0).astype(F32)
    dbias = jnp.where((kl // CHUNK) <= (ql // CHUNK), -2.0 * scal[2:2 + DF_HEADS, None, None] * after, NEG_BIG)
    blk = lambda: pl.BlockSpec((1, tb, DF_VD), lambda bi, h, i: (bi, i, h))
    return pl.pallas_call(
        functools.partial(_attn_prompt_kernel, tb=tb),
        grid=(b, DF_HEADS, nb),
        in_specs=[pl.BlockSpec(memory_space=pltpu.SMEM), blk(),
                  pl.BlockSpec((1, nb, tb, DF_VD), lambda bi, h, i: (bi, 0, 0, h)),
                  _const_spec((nb, tb, V7X_LANES)),
                  pl.BlockSpec((1, nb, DF_VD, tb), lambda bi, h, i: (bi, 0, h, 0)),
                  _const_spec((1, DF_VD)),
                  pl.BlockSpec((1, tb, 2 * tb), lambda bi, h, i: (h, 0, 0))],
        out_specs=blk(),
        out_shape=jax.ShapeDtypeStruct((b, t, DF_V), BF16),
        scratch_shapes=[pltpu.VMEM((1, 2 * tb), F32),
                        pltpu.VMEM((DF_VD + ONES_ROWS, 2 * tb), F32),
                        pltpu.VMEM((tb, 2 * tb), F32),
                        pltpu.VMEM((1, 2 * tb), F32)],
        compiler_params=pltpu.CompilerParams(
            dimension_semantics=("parallel", "parallel", "arbitrary"), vmem_limit_bytes=32 << 20),
        name="attn_prompt",
    )(scal, q, k, kpos, vt, g, dbias)


def _attn_sample_kernel(sc_ref, q_ref, kn_ref, vn_ref, kp_ref, vp_ref, g_ref, o_ref, *, t, past):
    h = pl.program_id(1)
    lam, out_scale, slope = sc_ref[0], sc_ref[1], sc_ref[2 + h]
    qs = _stack_maps(q_ref[0])

    def scores(keys, k0):
        nk = keys.shape[0]
        s = lax.dot_general(qs, keys.astype(BF16), NT, preferred_element_type=F32)
        qpos = past + lax.rem(lax.broadcasted_iota(jnp.int32, (2 * t, nk), 0), t)
        kpos = k0 + lax.broadcasted_iota(jnp.int32, (2 * t, nk), 1)
        s = s - slope * jnp.abs(qpos - kpos).astype(F32)
        return jnp.where((kpos >> 6) <= (qpos >> 6), s, NEG_BIG)

    sp = scores(kp_ref[0], 0)
    sn = scores(kn_ref[0], past)
    m = jnp.maximum(jnp.max(sp, axis=-1, keepdims=True), jnp.max(sn, axis=-1, keepdims=True))
    pp = jnp.exp(sp - m)
    pn = jnp.exp(sn - m)
    l = jnp.sum(pp, axis=-1, keepdims=True) + jnp.sum(pn, axis=-1, keepdims=True)
    acc = (jnp.dot(pp.astype(BF16), vp_ref[0].astype(BF16), preferred_element_type=F32)
           + jnp.dot(pn.astype(BF16), vn_ref[0].astype(BF16), preferred_element_type=F32))
    on = acc / l
    o = on[0:t] - lam * on[t:2 * t]
    o_ref[0] = (_rms(o, g_ref[...], SUBLN_EPS) * out_scale).astype(o_ref.dtype)


def _attn_sample(scal, q, k, v, cache_k, cache_v, layer, g):
    b, t, _ = q.shape
    past = cache_k.shape[2]
    new = lambda: pl.BlockSpec((1, t, DF_VD), lambda bi, h: (bi, 0, h))
    old = lambda: pl.BlockSpec((None, 1, past, DF_VD), lambda bi, h: (layer, bi, 0, h))
    return pl.pallas_call(
        functools.partial(_attn_sample_kernel, t=t, past=past),
        grid=(b, DF_HEADS),
        in_specs=[pl.BlockSpec(memory_space=pltpu.SMEM), new(), new(), new(), old(), old(),
                  _const_spec((1, DF_VD))],
        out_specs=new(),
        out_shape=jax.ShapeDtypeStruct((b, t, DF_V), BF16),
        compiler_params=pltpu.CompilerParams(
            dimension_semantics=("parallel", "parallel"), vmem_limit_bytes=32 << 20),
        name="attn_sample",
    )(scal, q, k, v, cache_k, cache_v, g)


FF_STEP = 256


def _mlp_kernel(x_ref, y_ref, o_ref, wo_ref, g_ref, wg_ref, wu_ref, wd_ref, out_ref, h_s):
    x1 = (x_ref[...]
          + jnp.dot(y_ref[...], wo_ref[0:RW_W, :], preferred_element_type=F32)
          + jnp.dot(o_ref[...], wo_ref[RW_W:RW_W + DF_V, :], preferred_element_type=F32))
    out_ref[...] = x1
    h_s[...] = _rms(x1, g_ref[...], RMS_EPS).astype(BF16)
    for c in range(0, D_FF, FF_STEP):
        h2 = h_s[...]
        gt = jnp.dot(h2, wg_ref[:, c:c + FF_STEP], preferred_element_type=F32)
        up = jnp.dot(h2, wu_ref[:, c:c + FF_STEP], preferred_element_type=F32)
        act = (gt * jax.nn.sigmoid(gt) * up).astype(BF16)
        out_ref[...] += jnp.dot(act, wd_ref[c:c + FF_STEP, :], preferred_element_type=F32)


def _mlp(x2d, y_rw, o_df, wo, g, wg, wu, wd):
    n = x2d.shape[0]
    tm = _row_tile(n, 512)
    row = lambda w: pl.BlockSpec((tm, w), lambda i: (i, 0))
    return pl.pallas_call(
        _mlp_kernel,
        grid=(n // tm,),
        in_specs=[row(D_MODEL), row(RW_W), row(DF_V), _const_spec((RW_W + DF_V, D_MODEL)),
                  _const_spec((1, D_MODEL)), _const_spec((D_MODEL, D_FF)), _const_spec((D_MODEL, D_FF)),
                  _const_spec((D_FF, D_MODEL))],
        out_specs=row(D_MODEL),
        out_shape=jax.ShapeDtypeStruct((n, D_MODEL), F32),
        scratch_shapes=[pltpu.VMEM((tm, D_MODEL), BF16)],
        compiler_params=_vmem_limit(52 << 20),
        name="mlp",
    )(x2d, y_rw, o_df, wo, g.reshape(1, D_MODEL), wg, wu, wd)


def _norm_kernel(x_ref, g_ref, o_ref):
    o_ref[...] = _rms(x_ref[...], g_ref[...], RMS_EPS)


def _final_norm(x2d, g):
    n = x2d.shape[0]
    tm = _row_tile(n, 1024)
    row = pl.BlockSpec((tm, D_MODEL), lambda i: (i, 0))
    return pl.pallas_call(
        _norm_kernel,
        grid=(n // tm,),
        in_specs=[row, _const_spec((1, D_MODEL))],
        out_specs=row,
        out_shape=jax.ShapeDtypeStruct((n, D_MODEL), F32),
        name="final_norm",
    )(x2d, g.reshape(1, D_MODEL))


def _layer(x, shift0, s0_pairs, attend, lp, vt_block):
    b, t, _ = x.shape
    n = b * t
    zr, q, k, v, kb, vb = _inproj(x.reshape(n, D_MODEL), lp["norm1"], lp["w_in"], vt_block)
    zr = zr.reshape(b, t, RW_COLS)
    tp = -(-t // RW_CHUNK) * RW_CHUNK
    zr_p = zr if tp == t else jnp.pad(zr, ((0, 0), (0, tp - t), (0, 0)))
    y_rw, s_fin = _rwkv(zr_p, shift0, s0_pairs, lp, t)
    if tp != t:
        y_rw = y_rw[:, :t]
    o = attend(q.reshape(b, t, DF_QK), kb, vb)
    x_new = _mlp(x.reshape(n, D_MODEL), y_rw.reshape(n, RW_W), o.reshape(n, DF_V), lp["w_out"], lp["norm2"],
                 lp["wg"], lp["wu"], lp["wd"])
    return (x_new.reshape(b, t, D_MODEL), zr[:, t - 1:t], s_fin,
            k.reshape(b, t, DF_HEADS, 2 * DF_HD), v.reshape(b, t, DF_HEADS, DF_VD))


@jax.jit
def kernel(x_prompt, x_sample, cache_k, cache_v, state_wkv, state_shift, norm1_g, w_in, rw_mu, rw_w0, rw_w2,
           rw_a0, rw_a2, rw_g2, rw_k_k, rw_k_a, rw_r_k, rw_lnx_w, rw_lnx_b, df_lq1, df_lk1, df_lq2, df_lk2,
           df_subln_g, w_out, norm2_g, ffn_w_gate, ffn_w_up, ffn_w_down, final_g):
    depth = w_in.shape[0]
    bp, tp = x_prompt.shape[0], x_prompt.shape[1]
    bs, ts = x_sample.shape[0], x_sample.shape[1]
    past = cache_k.shape[2]
    tb = _row_tile(tp, ATTN_BLOCK)
    assert tb & (tb - 1) == 0 and tb % CHUNK == 0
    slopes = 2.0 ** (-8.0 * jnp.arange(1, DF_HEADS + 1, dtype=F32) / DF_HEADS)
    head = jnp.arange(RW_W) // RW_HD
    seg = (head[:, None] == head[None, :]).astype(BF16)
    zero_blk = jnp.zeros((DECAY_RANK, RW_W), F32)
    ck = cache_k.reshape(depth, bs, past, DF_QK)
    cv = cache_v.reshape(depth, bs, past, DF_V)
    w_in_b, w_out_b = w_in.astype(BF16), w_out.astype(BF16)
    wg_b, wu_b, wd_b = ffn_w_gate.astype(BF16), ffn_w_up.astype(BF16), ffn_w_down.astype(BF16)

    xp, xs = x_prompt, x_sample
    outs = [[] for _ in range(8)]
    for l in range(depth):
        lam_init = 0.8 - 0.6 * math.exp(-0.3 * l)
        lam = (jnp.exp(jnp.sum((df_lq1[l] * df_lk1[l]).astype(F32)))
               - jnp.exp(jnp.sum((df_lq2[l] * df_lk2[l]).astype(F32))) + lam_init)
        scal = jnp.concatenate([jnp.stack([lam, jnp.asarray(1.0 - lam_init, F32)]), slopes]).astype(F32)
        row = lambda u: u[l].reshape(1, -1)
        lp = dict(
            norm1=norm1_g[l], w_in=w_in_b[l], mu=row(rw_mu), w0=row(rw_w0), a0=row(rw_a0),
            w2a2=jnp.concatenate([jnp.concatenate([rw_w2[l], zero_blk], axis=1),
                                  jnp.concatenate([zero_blk, rw_a2[l]], axis=1)], axis=0).astype(BF16),
            g2=rw_g2[l].astype(BF16), k_k=row(rw_k_k), k_a=row(rw_k_a), r_k=row(rw_r_k),
            lnx_w=row(rw_lnx_w), lnx_b=row(rw_lnx_b), seg=seg, w_out=w_out_b[l], norm2=norm2_g[l],
            wg=wg_b[l], wu=wu_b[l], wd=wd_b[l])
        g_sub = df_subln_g[l].reshape(1, DF_VD)

        att_p = lambda q, kb, vt: _attn_prompt(scal, q, kb.reshape(bp, tp // tb, tb, DF_QK),
                                               vt.reshape(bp, tp // tb, DF_V, tb), g_sub, tb)
        xp, shp, sp, kp, vp = _layer(xp, jnp.zeros((bp, 1, RW_COLS), F32),
                                     jnp.zeros((bp, N_PAIRS, RW_PAIR, RW_PAIR), F32), att_p, lp, tb)
        att_s = lambda q, kb, vb: _attn_sample(scal, q, kb.reshape(bs, ts, DF_QK), vb.reshape(bs, ts, DF_V),
                                               ck, cv, l, g_sub)
        xs, shs, ss, kss, vss = _layer(xs, state_shift[l], _to_pairs(state_wkv[l]), att_s, lp, None)
        for lst, val in zip(outs, (kp, vp, _from_pairs(sp), shp, kss, vss, _from_pairs(ss), shs)):
            lst.append(val)

    y_prompt = _final_norm(xp.reshape(-1, D_MODEL), final_g).reshape(xp.shape)
    y_sample = _final_norm(xs.reshape(-1, D_MODEL), final_g).reshape(xs.shape)
    return (y_prompt, y_sample) + tuple(jnp.stack(lst) for lst in outs)
```

```python
import functools
import math

import jax
import jax.numpy as jnp
from jax import lax
from jax.experimental import pallas as pl
from jax.experimental.pallas import tpu as pltpu

F32 = jnp.float32
BF16 = jnp.bfloat16

D_MODEL = 1024
CHUNK = 64
RW_HEADS = 8
RW_HD = 64
RW_W = RW_HEADS * RW_HD
DECAY_RANK = 64
AAA_RANK = 64
GATE_RANK = 128
RW_COLS = 3 * RW_W + DECAY_RANK + AAA_RANK + GATE_RANK
DF_HEADS = 4
DF_HD = 64
DF_VD = 2 * DF_HD
DF_QK = DF_HEADS * 2 * DF_HD
DF_V = DF_HEADS * DF_VD
P_IN = RW_COLS + 2 * DF_QK + DF_V
D_FF = 2816
RMS_EPS = 1e-6
LNX_EPS = 64e-5
SUBLN_EPS = 1e-5
QK_SCALE = DF_HD ** -0.5

V7X_LANES = 128
V7X_MXU_COLS = 256
ATTN_BLOCK = 512
POS_SPLIT = 256
ONES_ROWS = 16
RW_CHUNK = 64
RW_TILE = 512
RW_SUB = 128
RW_PAIR = 2 * RW_HD
N_PAIRS = RW_HEADS // 2
NEG_BIG = -1e30

INV_PASSES = 1
CHUNK_PASSES = 1
STATE_PASSES = 1
SEG_PASSES = 1
CUM_PIECES = 2

VPU_STAGE, MXU_STAGE = "vpu", "mxu"

NN = (((1,), (0,)), ((), ()))
NT = (((1,), (1,)), ((), ()))
TN = (((0,), (0,)), ((), ()))


def _vmem_limit(nbytes):
    return pltpu.CompilerParams(vmem_limit_bytes=int(nbytes))


def _row_tile(n, target):
    t = min(n, target)
    while n % t:
        t -= 8
    return t


def _const_spec(shape):
    zeros = (0,) * len(shape)
    return pl.BlockSpec(shape, lambda *_: zeros, pipeline_mode=pl.Buffered(1))


def _split(x, parts):
    out, rem = [], x
    for i in range(parts):
        hi = rem.astype(BF16)
        out.append(hi)
        if i + 1 < parts:
            rem = rem - hi.astype(F32)
    return out


def _mm(a, b, dims=NN, passes=1):
    dot = functools.partial(lax.dot_general, dimension_numbers=dims, preferred_element_type=F32)
    if passes == 1:
        return dot(a.astype(BF16), b.astype(BF16))
    ah, al = _split(a, 2)
    if passes == 2:
        bh = b.astype(BF16)
        return dot(ah, bh) + dot(al, bh)
    bh, bl = _split(b, 2)
    return dot(ah, bh) + dot(al, bh) + dot(ah, bl)


def _rms(x, g, eps):
    return x * lax.rsqrt(jnp.mean(x * x, axis=-1, keepdims=True) + eps) * g


def _inproj_kernel(x_ref, g_ref, w_ref, *refs, vt_block):
    zr_ref, q_ref, k_ref, v_ref, kb_ref, vb_ref = refs[-6:]
    h = _rms(x_ref[...], g_ref[...], RMS_EPS).astype(BF16)
    step = V7X_MXU_COLS

    def proj(c0):
        return jnp.dot(h, w_ref[:, c0:c0 + step], preferred_element_type=F32)

    for c in range(0, RW_COLS, step):
        zr_ref[:, c:c + step] = proj(c)
    for c in range(0, DF_QK, step):
        q_ref[:, c:c + step] = (proj(RW_COLS + c) * QK_SCALE).astype(q_ref.dtype)
        kc = proj(RW_COLS + DF_QK + c)
        for hh in range(step // DF_VD):
            k_ref[:, c // DF_VD + hh, :] = kc[:, hh * DF_VD:(hh + 1) * DF_VD]
        kb_ref[:, c:c + step] = kc.astype(kb_ref.dtype)
        vc = proj(RW_COLS + 2 * DF_QK + c)
        for hh in range(step // DF_VD):
            v_ref[:, c // DF_VD + hh, :] = vc[:, hh * DF_VD:(hh + 1) * DF_VD]
        if vt_block is None:
            vb_ref[:, c:c + step] = vc.astype(vb_ref.dtype)
        else:
            for rb in range(vb_ref.shape[0]):
                vb_ref[rb, c:c + step, :] = vc[rb * vt_block:(rb + 1) * vt_block, :].T.astype(vb_ref.dtype)


def _inproj(x2d, g, w_bf, vt_block, layer, depth, kv_all):
    n = x2d.shape[0]
    tm = _row_tile(n, 512)
    row = lambda w: pl.BlockSpec((tm, w), lambda i: (i, 0))
    heads = lambda: pl.BlockSpec((None, tm, DF_HEADS, DF_VD), lambda i: (layer, i, 0, 0))
    carried = [] if kv_all is None else list(kv_all)
    if vt_block is None:
        vb_spec, vb_shape = row(DF_V), (n, DF_V)
    else:
        assert tm % vt_block == 0
        vb_spec = pl.BlockSpec((tm // vt_block, DF_V, vt_block), lambda i: (i, 0, 0))
        vb_shape = (n // vt_block, DF_V, vt_block)
    return pl.pallas_call(
        functools.partial(_inproj_kernel, vt_block=vt_block),
        grid=(n // tm,),
        in_specs=[row(D_MODEL), _const_spec((1, D_MODEL)), _const_spec((D_MODEL, P_IN))]
        + [pl.BlockSpec(memory_space=pl.ANY) for _ in carried],
        out_specs=[row(RW_COLS), row(DF_QK), heads(), heads(), row(DF_QK), vb_spec],
        out_shape=[
            jax.ShapeDtypeStruct((n, RW_COLS), F32),
            jax.ShapeDtypeStruct((n, DF_QK), BF16),
            jax.ShapeDtypeStruct((depth, n, DF_HEADS, 2 * DF_HD), F32),
            jax.ShapeDtypeStruct((depth, n, DF_HEADS, DF_VD), F32),
            jax.ShapeDtypeStruct((n, DF_QK), BF16),
            jax.ShapeDtypeStruct(vb_shape, BF16),
        ],
        input_output_aliases={3: 2, 4: 3} if carried else {},
        compiler_params=_vmem_limit(48 << 20),
        name="inproj",
    )(x2d, g.reshape(1, D_MODEL), w_bf, *carried)


def _neumann_inverse(dab, passes):
    n = dab[0].shape[0]
    ii = lax.broadcasted_iota(jnp.int32, (n, n), 0)
    jj = lax.broadcasted_iota(jnp.int32, (n, n), 1)
    same16 = (ii >> 4) == (jj >> 4)
    same32 = (ii >> 5) == (jj >> 5)
    eye = jnp.where(ii == jj, 1.0, 0.0)
    mm = functools.partial(_mm, passes=passes)
    x = [jnp.where(same16, d, 0.0) for d in dab]
    p = [eye + xi for xi in x]
    x = [mm(xi, xi) for xi in x]
    yield MXU_STAGE
    for _ in range(2):
        xp = [mm(xi, jnp.concatenate([xi, pi], axis=1)) for xi, pi in zip(x, p)]
        p = [pi + r[:, n:] for pi, r in zip(p, xp)]
        x = [r[:, :n] for r in xp]
        yield MXU_STAGE
    p = [pi + mm(xi, pi) for xi, pi in zip(x, p)]
    yield MXU_STAGE
    for off_mask in (same32 & jnp.logical_not(same16), jnp.logical_not(same32)):
        off = [jnp.where(off_mask, d, 0.0) for d in dab]
        po = [mm(pi, oi) for pi, oi in zip(p, off)]
        yield MXU_STAGE
        p = [pi + mm(qi, pi) for pi, qi in zip(p, po)]
        yield MXU_STAGE
    return p


def _rwkv_rows(zr_ref, mu_ref, w0_ref, a0_ref, w2a2_ref, g2_ref, kk_ref, ka_ref, rk_ref, lnw_ref, lnb_ref, seg_ref,
               y_ref, s_s, prev_s, r0, tt, t_valid, order):
    zr = zr_ref[0, r0:r0 + tt, :]
    row = lax.broadcasted_iota(jnp.int32, (tt, 1), 0)
    before = prev_s[0:1, :] if r0 == 0 else zr_ref[0, r0 - 1:r0, :]
    zprev = jnp.where(row == 0, before, pltpu.roll(zr, 1, axis=0))
    zm = zr + (zprev - zr) * mu_ref[...]
    yield VPU_STAGE
    r = zm[:, 0:RW_W]
    k = zm[:, RW_W:2 * RW_W]
    v = zm[:, 2 * RW_W:3 * RW_W]
    xwa = zm[:, 3 * RW_W:3 * RW_W + DECAY_RANK + AAA_RANK]
    gd = zm[:, 3 * RW_W + DECAY_RANK + AAA_RANK:RW_COLS]
    lane = lax.broadcasted_iota(jnp.int32, xwa.shape, 1)
    xwa = jnp.where(lane < DECAY_RANK, jnp.tanh(xwa), xwa)
    lora = jnp.dot(xwa.astype(BF16), w2a2_ref[...], preferred_element_type=F32)
    yield VPU_STAGE
    lw = -math.exp(-0.5) * jax.nn.sigmoid(w0_ref[...] + lora[:, 0:RW_W])
    a = jax.nn.sigmoid(a0_ref[...] + lora[:, RW_W:2 * RW_W])
    gate = jnp.dot(jax.nn.sigmoid(gd).astype(BF16), g2_ref[...], preferred_element_type=F32)
    yield VPU_STAGE

    def segsum(x):
        return sum(jnp.dot(part, seg_ref[...], preferred_element_type=F32) for part in _split(x, SEG_PASSES))

    kkn = k * kk_ref[...]
    kkn = kkn * lax.rsqrt(jnp.maximum(segsum(kkn * kkn), 1e-24))
    yield VPU_STAGE
    k2 = k * (1.0 + (a - 1.0) * ka_ref[...])
    if t_valid < tt:
        valid = row < t_valid
        lw = jnp.where(valid, lw, 0.0)
        kkn = jnp.where(valid, kkn, 0.0)
        k2 = jnp.where(valid, k2, 0.0)
        v = jnp.where(valid, v, 0.0)

    c = RW_CHUNK
    ti = lax.broadcasted_iota(jnp.int32, (tt, tt), 0)
    tj = lax.broadcasted_iota(jnp.int32, (tt, tt), 1)
    cum = jnp.where(((ti >> 6) == (tj >> 6)) & (ti >= tj), 1.0, 0.0).astype(BF16)
    g = sum(jnp.dot(cum, part, preferred_element_type=F32) for part in _split(lw, CUM_PIECES))
    gc = jnp.concatenate([jnp.broadcast_to(g[e - 1:e, :], (c, RW_W)) for e in range(c, tt + 1, c)], axis=0)
    yield VPU_STAGE
    ieg = jnp.exp(-g)
    egc = jnp.exp(gc - g)
    b = kkn * a
    at = -kkn * jnp.exp(g - lw)
    rt = r * jnp.exp(g)
    yield VPU_STAGE
    bh, kh = b * ieg, k2 * ieg
    bb, kb = b * egc, k2 * egc
    gam = jnp.exp(gc)
    yield VPU_STAGE

    probs = [(ci, p) for ci in range(tt // c) for p in range(N_PAIRS)]

    def blk(x, ci, p):
        return x[ci * c:(ci + 1) * c, p * RW_PAIR:(p + 1) * RW_PAIR]

    head0 = lax.broadcasted_iota(jnp.int32, (c, RW_PAIR), 1) < RW_HD

    def stack(x):
        return jnp.concatenate([jnp.where(head0, x, 0.0), jnp.where(head0, 0.0, x)], axis=0)

    def twice(x):
        return jnp.concatenate([x, x], axis=0)

    n = 2 * c
    ii = lax.broadcasted_iota(jnp.int32, (n, n), 0)
    jj = lax.broadcasted_iota(jnp.int32, (n, n), 1)
    same_head = (ii >> 6) == (jj >> 6)
    strict = same_head & (ii > jj)
    incl = same_head & (ii >= jj)

    ats = [stack(blk(at, *q)) for q in probs]
    rts = [stack(blk(rt, *q)) for q in probs]
    vs = [stack(blk(v, *q)) for q in probs]
    aa = [_mm(jnp.concatenate([a_, r_], axis=0),
              jnp.concatenate([twice(blk(bh, *q)), twice(blk(kh, *q))], axis=0), NT, CHUNK_PASSES)
          for a_, r_, q in zip(ats, rts, probs)]
    yield MXU_STAGE
    dab = [jnp.where(strict, x[0:n, 0:n], 0.0) for x in aa]
    dak = [jnp.where(strict, x[0:n, n:2 * n], 0.0) for x in aa]
    drbk = [jnp.concatenate([jnp.where(incl, x[n:2 * n, 0:n], 0.0),
                             jnp.where(incl, x[n:2 * n, n:2 * n], 0.0)], axis=1) for x in aa]
    dv = [_mm(d, v_, NN, CHUNK_PASSES) for d, v_ in zip(dak, vs)]
    tinv = yield from _neumann_inverse(dab, INV_PASSES)
    gh = [_mm(t_, jnp.concatenate([a_, d], axis=1), NN, INV_PASSES) for t_, a_, d in zip(tinv, ats, dv)]
    yield MXU_STAGE
    zeros = jnp.zeros((n, n), F32)
    qy = [_mm(d, jnp.concatenate([x, jnp.concatenate([zeros, v_], axis=1)], axis=0), NN, CHUNK_PASSES)
          for d, x, v_ in zip(drbk, gh, vs)]
    yield MXU_STAGE
    qe = [r_ + x[:, 0:n] for r_, x in zip(rts, qy)]
    y0 = [x[:, n:2 * n] for x in qy]
    bbs = [stack(blk(bb, *q)) for q in probs]
    kbs = [stack(blk(kb, *q)) for q in probs]
    mz = [_mm(jnp.concatenate([x, jnp.concatenate([zeros, v_], axis=1)], axis=0),
              jnp.concatenate([b_, k_], axis=0), TN, STATE_PASSES)
          for x, v_, b_, k_ in zip(gh, vs, bbs, kbs)]
    yield MXU_STAGE
    eye = ii == jj
    mx = [jnp.where(eye, blk(gam, *q)[0:1, :], 0.0) + x[0:n] for x, q in zip(mz, probs)]
    zz = [x[n:2 * n] for x in mz]

    assert order[0] == r0, "sub-tiles must reach their state update in time order"
    state = [s_s[p] for p in range(N_PAIRS)]
    y_rows = []
    for ci in range(tt // c):
        y_pairs = []
        for p in range(N_PAIRS):
            i = ci * N_PAIRS + p
            ys = _mm(qe[i], state[p], NT, STATE_PASSES) + y0[i]
            y_pairs.append(ys[0:c] + ys[c:n])
            state[p] = _mm(state[p], mx[i], NN, STATE_PASSES) + zz[i]
        y_rows.append(jnp.concatenate(y_pairs, axis=1))
    for p in range(N_PAIRS):
        s_s[p] = state[p]
    order[0] = r0 + tt
    yield MXU_STAGE
    y = jnp.concatenate(y_rows, axis=0)

    inv_hd = 1.0 / RW_HD
    d = y - segsum(y) * inv_hd
    yield VPU_STAGE
    yn = d * lax.rsqrt(segsum(d * d) * inv_hd + LNX_EPS) * lnw_ref[...] + lnb_ref[...]
    yn = yn + segsum(r * k2 * rk_ref[...]) * v
    y_ref[0, r0:r0 + tt, :] = (yn * gate).astype(y_ref.dtype)


def _rwkv_kernel(zr_ref, shift_ref, s0_ref, mu_ref, w0_ref, a0_ref, w2a2_ref, g2_ref, kk_ref, ka_ref,
                 rk_ref, lnw_ref, lnb_ref, seg_ref, y_ref, sfin_ref, s_s, prev_s, *, tt, sub, t_valid):
    t = pl.program_id(1)

    @pl.when(t == 0)
    def _():
        s_s[...] = s0_ref[0]
        prev_s[0:1, :] = shift_ref[0]

    order = [0]
    pending = [_rwkv_rows(zr_ref, mu_ref, w0_ref, a0_ref, w2a2_ref, g2_ref, kk_ref, ka_ref, rk_ref, lnw_ref,
                          lnb_ref, seg_ref, y_ref, s_s, prev_s, r0, sub, t_valid, order)
               for r0 in range(0, tt, sub)]
    active = []
    while pending or active:
        if pending and all(stage == MXU_STAGE for _, stage in active):
            active.append([pending.pop(0), VPU_STAGE])
        for item in list(active):
            try:
                item[1] = next(item[0])
            except StopIteration:
                active.remove(item)
    prev_s[0:1, :] = zr_ref[0, tt - 1:tt, :]

    @pl.when(t == pl.num_programs(1) - 1)
    def _():
        sfin_ref[0] = s_s[...]


def _rwkv(zr, shift0, s0_pairs, lp, t_valid):
    b, tp, _ = zr.shape
    tt = _row_tile(tp, RW_TILE)
    assert t_valid == tp or tp == tt, "padding is only supported inside a single time tile"
    vec = lambda w: _const_spec((1, w))
    return pl.pallas_call(
        functools.partial(_rwkv_kernel, tt=tt, sub=min(tt, RW_SUB), t_valid=tt if t_valid == tp else t_valid),
        grid=(b, tp // tt),
        in_specs=[
            pl.BlockSpec((1, tt, RW_COLS), lambda i, t: (i, t, 0)),
            pl.BlockSpec((1, 1, RW_COLS), lambda i, t: (i, 0, 0)),
            pl.BlockSpec((1, N_PAIRS, RW_PAIR, RW_PAIR), lambda i, t: (i, 0, 0, 0)),
            vec(RW_COLS), vec(RW_W), vec(RW_W),
            _const_spec((DECAY_RANK + AAA_RANK, 2 * RW_W)), _const_spec((GATE_RANK, RW_W)),
            vec(RW_W), vec(RW_W), vec(RW_W), vec(RW_W), vec(RW_W),
            _const_spec((RW_W, RW_W)),
        ],
        out_specs=[
            pl.BlockSpec((1, tt, RW_W), lambda i, t: (i, t, 0)),
            pl.BlockSpec((1, N_PAIRS, RW_PAIR, RW_PAIR), lambda i, t: (i, 0, 0, 0)),
        ],
        out_shape=[
            jax.ShapeDtypeStruct((b, tp, RW_W), BF16),
            jax.ShapeDtypeStruct((b, N_PAIRS, RW_PAIR, RW_PAIR), F32),
        ],
        scratch_shapes=[
            pltpu.VMEM((N_PAIRS, RW_PAIR, RW_PAIR), F32),
            pltpu.VMEM((8, RW_COLS), F32),
        ],
        compiler_params=pltpu.CompilerParams(
            dimension_semantics=("parallel", "arbitrary"), vmem_limit_bytes=48 << 20),
        name="rwkv",
    )(zr, shift0, s0_pairs, lp["mu"], lp["w0"], lp["a0"], lp["w2a2"], lp["g2"], lp["k_k"], lp["k_a"],
      lp["r_k"], lp["lnx_w"], lp["lnx_b"], lp["seg"])


def _to_pairs(s):
    b = s.shape[0]
    s = s.reshape(b, N_PAIRS, 2, RW_HD, RW_HD)
    eye = jnp.eye(2, dtype=s.dtype)
    return jnp.einsum("bphvk,hg->bphvgk", s, eye).reshape(b, N_PAIRS, RW_PAIR, RW_PAIR)


def _from_pairs(sp):
    b = sp.shape[0]
    sp = sp.reshape(b, N_PAIRS, 2, RW_HD, 2, RW_HD)
    return jnp.stack([sp[:, :, 0, :, 0, :], sp[:, :, 1, :, 1, :]], axis=2).reshape(b, RW_HEADS, RW_HD, RW_HD)


def _stack_maps(q):
    lane = lax.broadcasted_iota(jnp.int32, q.shape, 1)
    zero = jnp.zeros_like(q)
    return jnp.concatenate([jnp.where(lane < DF_HD, q, zero), jnp.where(lane < DF_HD, zero, q)], axis=0)


def _attn_prompt_kernel(sc_ref, q_ref, k_ref, kpos_ref, vt_ref, g_ref, dbias_ref, o_ref, m_s, acc_s, s_s, smax_s,
                        *, tb):
    h = pl.program_id(1)
    i = pl.program_id(2)
    lam, out_scale, slope = sc_ref[0], sc_ref[1], sc_ref[2 + h]
    qt = q_ref[0].astype(F32).T
    dim = lax.broadcasted_iota(jnp.int32, (DF_VD, tb), 0)
    q_pos = jnp.where(dim == 0, slope * POS_SPLIT, jnp.where(dim == 1, slope, 0.0))
    qs_t = jnp.concatenate(
        [jnp.concatenate([jnp.where(dim < DF_HD, qt, 0.0), jnp.where(dim < DF_HD, 0.0, qt)], axis=1),
         jnp.concatenate([q_pos, q_pos], axis=1)], axis=0).astype(BF16)

    def scores(j):
        keys = jnp.concatenate([k_ref[0, j], kpos_ref[j]], axis=1)
        return jnp.dot(keys, qs_t, preferred_element_type=F32)

    ones = jnp.ones((ONES_ROWS, tb), BF16)

    def weighted_values(j, p):
        return jnp.dot(jnp.concatenate([vt_ref[0, j], ones], axis=0), p, preferred_element_type=F32)

    def stash_scores(j):
        sc = scores(j)
        s_s[...] = sc
        smax_s[...] = jnp.max(sc, axis=0, keepdims=True)

    stash_scores(0)
    s = scores(i) + dbias_ref[0]
    m = jnp.max(s, axis=0, keepdims=True)
    m_s[...] = m
    acc_s[...] = weighted_values(i, jnp.exp(s - m).astype(BF16))

    def step(j, stash_next):
        s = s_s[...]
        m_old = m_s[...]
        m_new = jnp.maximum(m_old, smax_s[...])
        if stash_next:
            stash_scores(j + 1)
        p = jnp.exp(s - m_new).astype(BF16)
        acc_s[...] = jnp.exp(m_old - m_new) * acc_s[...] + weighted_values(j, p)
        m_s[...] = m_new

    n_lead = jnp.maximum(i - 1, 0)

    def two_steps(jj, carry):
        step(2 * jj, True)
        step(2 * jj + 1, True)
        return carry

    lax.fori_loop(0, n_lead // 2, two_steps, 0)

    @pl.when(n_lead % 2 == 1)
    def _():
        step(n_lead - 1, True)

    @pl.when(i > 0)
    def _():
        step(i - 1, False)

    acc = acc_s[...]
    on = acc[0:DF_VD] / acc[DF_VD:DF_VD + 1]
    ot = on[:, 0:tb] - lam * on[:, tb:2 * tb]
    ot = ot * lax.rsqrt(jnp.mean(ot * ot, axis=0, keepdims=True) + SUBLN_EPS)
    o_ref[0] = (ot.T * g_ref[...] * out_scale).astype(o_ref.dtype)


def _attn_prompt(scal, q, k, vt, g, tb):
    b, t, _ = q.shape
    nb = t // tb
    pos = jnp.arange(t, dtype=jnp.int32).reshape(nb, tb, 1)
    lane = jnp.arange(V7X_LANES, dtype=jnp.int32)
    kpos = jnp.where(lane == 0, pos // POS_SPLIT, jnp.where(lane == 1, pos % POS_SPLIT, 0)).astype(BF16)
    kl = jnp.arange(tb, dtype=jnp.int32)[:, None]
    ql = jnp.arange(2 * tb, dtype=jnp.int32)[None, :] % tb
    after = jnp.maximum(kl - ql, 0).astype(F32)
    dbias = jnp.where((kl // CHUNK) <= (ql // CHUNK), -2.0 * scal[2:2 + DF_HEADS, None, None] * after, NEG_BIG)
    blk = lambda: pl.BlockSpec((1, tb, DF_VD), lambda bi, h, i: (bi, i, h))
    return pl.pallas_call(
        functools.partial(_attn_prompt_kernel, tb=tb),
        grid=(b, DF_HEADS, nb),
        in_specs=[pl.BlockSpec(memory_space=pltpu.SMEM), blk(),
                  pl.BlockSpec((1, nb, tb, DF_VD), lambda bi, h, i: (bi, 0, 0, h)),
                  _const_spec((nb, tb, V7X_LANES)),
                  pl.BlockSpec((1, nb, DF_VD, tb), lambda bi, h, i: (bi, 0, h, 0)),
                  _const_spec((1, DF_VD)),
                  pl.BlockSpec((1, tb, 2 * tb), lambda bi, h, i: (h, 0, 0))],
        out_specs=blk(),
        out_shape=jax.ShapeDtypeStruct((b, t, DF_V), BF16),
        scratch_shapes=[pltpu.VMEM((1, 2 * tb), F32),
                        pltpu.VMEM((DF_VD + ONES_ROWS, 2 * tb), F32),
                        pltpu.VMEM((tb, 2 * tb), F32),
                        pltpu.VMEM((1, 2 * tb), F32)],
        compiler_params=pltpu.CompilerParams(
            dimension_semantics=("parallel", "parallel", "arbitrary"), vmem_limit_bytes=32 << 20),
        name="attn_prompt",
    )(scal, q, k, kpos, vt, g, dbias)


def _attn_sample_kernel(sc_ref, q_ref, kn_ref, vn_ref, kp_ref, vp_ref, g_ref, o_ref, *, t, past):
    lam, out_scale = sc_ref[0], sc_ref[1]
    for h in range(DF_HEADS):
        slope = sc_ref[2 + h]
        lanes = slice(h * DF_VD, (h + 1) * DF_VD)
        qs = _stack_maps(q_ref[0, :, lanes])

        def scores(keys, k0):
            nk = keys.shape[0]
            s = lax.dot_general(qs, keys.astype(BF16), NT, preferred_element_type=F32)
            qpos = past + lax.rem(lax.broadcasted_iota(jnp.int32, (2 * t, nk), 0), t)
            kpos = k0 + lax.broadcasted_iota(jnp.int32, (2 * t, nk), 1)
            s = s - slope * jnp.abs(qpos - kpos).astype(F32)
            return jnp.where((kpos >> 6) <= (qpos >> 6), s, NEG_BIG)

        sp = scores(kp_ref[0, :, h, :], 0)
        sn = scores(kn_ref[0, :, lanes], past)
        m = jnp.maximum(jnp.max(sp, axis=-1, keepdims=True), jnp.max(sn, axis=-1, keepdims=True))
        pp = jnp.exp(sp - m)
        pn = jnp.exp(sn - m)
        l = jnp.sum(pp, axis=-1, keepdims=True) + jnp.sum(pn, axis=-1, keepdims=True)
        acc = (jnp.dot(pp.astype(BF16), vp_ref[0, :, h, :].astype(BF16), preferred_element_type=F32)
               + jnp.dot(pn.astype(BF16), vn_ref[0, :, lanes].astype(BF16), preferred_element_type=F32))
        on = acc / l
        o = on[0:t] - lam * on[t:2 * t]
        o_ref[0, :, lanes] = (_rms(o, g_ref[...], SUBLN_EPS) * out_scale).astype(o_ref.dtype)


def _attn_sample(scal, q, k, v, cache_k, cache_v, layer, g):
    b, t, _ = q.shape
    past = cache_k.shape[2]
    new = lambda: pl.BlockSpec((1, t, DF_V), lambda bi: (bi, 0, 0))
    old = lambda: pl.BlockSpec((None, 1, past, DF_HEADS, DF_VD), lambda bi: (layer, bi, 0, 0, 0))
    return pl.pallas_call(
        functools.partial(_attn_sample_kernel, t=t, past=past),
        grid=(b,),
        in_specs=[pl.BlockSpec(memory_space=pltpu.SMEM), new(), new(), new(), old(), old(),
                  _const_spec((1, DF_VD))],
        out_specs=new(),
        out_shape=jax.ShapeDtypeStruct((b, t, DF_V), BF16),
        compiler_params=pltpu.CompilerParams(dimension_semantics=("parallel",), vmem_limit_bytes=48 << 20),
        name="attn_sample",
    )(scal, q, k, v, cache_k, cache_v, g)


FF_STEP = 256


def _mlp_kernel(x_ref, y_ref, o_ref, wo_ref, g_ref, wg_ref, wu_ref, wd_ref, gf_ref, out_ref, h_s, *, final_norm):
    x1 = (x_ref[...]
          + jnp.dot(y_ref[...], wo_ref[0:RW_W, :], preferred_element_type=F32)
          + jnp.dot(o_ref[...], wo_ref[RW_W:RW_W + DF_V, :], preferred_element_type=F32))
    out_ref[...] = x1
    h_s[...] = _rms(x1, g_ref[...], RMS_EPS).astype(BF16)
    for c in range(0, D_FF, FF_STEP):
        h2 = h_s[...]
        gt = jnp.dot(h2, wg_ref[:, c:c + FF_STEP], preferred_element_type=F32)
        up = jnp.dot(h2, wu_ref[:, c:c + FF_STEP], preferred_element_type=F32)
        act = (gt * jax.nn.sigmoid(gt) * up).astype(BF16)
        out_ref[...] += jnp.dot(act, wd_ref[c:c + FF_STEP, :], preferred_element_type=F32)
    if final_norm:
        out_ref[...] = _rms(out_ref[...], gf_ref[...], RMS_EPS)


def _mlp(x2d, y_rw, o_df, wo, g, wg, wu, wd, final_g, final_norm):
    n = x2d.shape[0]
    tm = _row_tile(n, 512)
    row = lambda w: pl.BlockSpec((tm, w), lambda i: (i, 0))
    return pl.pallas_call(
        functools.partial(_mlp_kernel, final_norm=final_norm),
        grid=(n // tm,),
        in_specs=[row(D_MODEL), row(RW_W), row(DF_V), _const_spec((RW_W + DF_V, D_MODEL)),
                  _const_spec((1, D_MODEL)), _const_spec((D_MODEL, D_FF)), _const_spec((D_MODEL, D_FF)),
                  _const_spec((D_FF, D_MODEL)), _const_spec((1, D_MODEL))],
        out_specs=row(D_MODEL),
        out_shape=jax.ShapeDtypeStruct((n, D_MODEL), F32),
        scratch_shapes=[pltpu.VMEM((tm, D_MODEL), BF16)],
        compiler_params=_vmem_limit(52 << 20),
        name="mlp",
    )(x2d, y_rw, o_df, wo, g.reshape(1, D_MODEL), wg, wu, wd, final_g.reshape(1, D_MODEL))


def _layer(x, shift0, s0_pairs, attend, lp, vt_block, layer, depth, kv_all):
    b, t, _ = x.shape
    n = b * t
    zr, q, k_all, v_all, kb, vb = _inproj(x.reshape(n, D_MODEL), lp["norm1"], lp["w_in"], vt_block, layer, depth,
                                          kv_all)
    zr = zr.reshape(b, t, RW_COLS)
    tp = -(-t // RW_CHUNK) * RW_CHUNK
    zr_p = zr if tp == t else jnp.pad(zr, ((0, 0), (0, tp - t), (0, 0)))
    y_rw, s_fin = _rwkv(zr_p, shift0, s0_pairs, lp, t)
    if tp != t:
        y_rw = y_rw[:, :t]
    o = attend(q.reshape(b, t, DF_QK), kb, vb)
    x_new = _mlp(x.reshape(n, D_MODEL), y_rw.reshape(n, RW_W), o.reshape(n, DF_V), lp["w_out"], lp["norm2"],
                 lp["wg"], lp["wu"], lp["wd"], lp["final_g"], layer == depth - 1)
    return x_new.reshape(b, t, D_MODEL), zr[:, t - 1:t], s_fin, (k_all, v_all)


@jax.jit
def kernel(x_prompt, x_sample, cache_k, cache_v, state_wkv, state_shift, norm1_g, w_in, rw_mu, rw_w0, rw_w2,
           rw_a0, rw_a2, rw_g2, rw_k_k, rw_k_a, rw_r_k, rw_lnx_w, rw_lnx_b, df_lq1, df_lk1, df_lq2, df_lk2,
           df_subln_g, w_out, norm2_g, ffn_w_gate, ffn_w_up, ffn_w_down, final_g):
    depth = w_in.shape[0]
    bp, tp = x_prompt.shape[0], x_prompt.shape[1]
    bs, ts = x_sample.shape[0], x_sample.shape[1]
    past = cache_k.shape[2]
    tb = _row_tile(tp, ATTN_BLOCK)
    assert tb & (tb - 1) == 0 and tb % CHUNK == 0
    slopes = 2.0 ** (-8.0 * jnp.arange(1, DF_HEADS + 1, dtype=F32) / DF_HEADS)
    head = jnp.arange(RW_W) // RW_HD
    seg = (head[:, None] == head[None, :]).astype(BF16)
    zero_blk = jnp.zeros((DECAY_RANK, RW_W), F32)
    w_in_b, w_out_b = w_in.astype(BF16), w_out.astype(BF16)
    wg_b, wu_b, wd_b = ffn_w_gate.astype(BF16), ffn_w_up.astype(BF16), ffn_w_down.astype(BF16)

    xp, xs = x_prompt, x_sample
    kv_p = kv_s = None
    outs = [[] for _ in range(4)]
    for l in range(depth):
        lam_init = 0.8 - 0.6 * math.exp(-0.3 * l)
        lam = (jnp.exp(jnp.sum((df_lq1[l] * df_lk1[l]).astype(F32)))
               - jnp.exp(jnp.sum((df_lq2[l] * df_lk2[l]).astype(F32))) + lam_init)
        scal = jnp.concatenate([jnp.stack([lam, jnp.asarray(1.0 - lam_init, F32)]), slopes]).astype(F32)
        row = lambda u: u[l].reshape(1, -1)
        lp = dict(
            norm1=norm1_g[l], w_in=w_in_b[l], mu=row(rw_mu), w0=row(rw_w0), a0=row(rw_a0),
            w2a2=jnp.concatenate([jnp.concatenate([rw_w2[l], zero_blk], axis=1),
                                  jnp.concatenate([zero_blk, rw_a2[l]], axis=1)], axis=0).astype(BF16),
            g2=rw_g2[l].astype(BF16), k_k=row(rw_k_k), k_a=row(rw_k_a), r_k=row(rw_r_k),
            lnx_w=row(rw_lnx_w), lnx_b=row(rw_lnx_b), seg=seg, w_out=w_out_b[l], norm2=norm2_g[l],
            wg=wg_b[l], wu=wu_b[l], wd=wd_b[l], final_g=final_g)
        g_sub = df_subln_g[l].reshape(1, DF_VD)

        att_p = lambda q, kb, vt: _attn_prompt(scal, q, kb.reshape(bp, tp // tb, tb, DF_QK),
                                               vt.reshape(bp, tp // tb, DF_V, tb), g_sub, tb)
        xp, shp, sp, kv_p = _layer(xp, jnp.zeros((bp, 1, RW_COLS), F32),
                                   jnp.zeros((bp, N_PAIRS, RW_PAIR, RW_PAIR), F32), att_p, lp, tb, l, depth, kv_p)
        att_s = lambda q, kb, vb: _attn_sample(scal, q, kb.reshape(bs, ts, DF_QK), vb.reshape(bs, ts, DF_V),
                                               cache_k, cache_v, l, g_sub)
        xs, shs, ss, kv_s = _layer(xs, state_shift[l], _to_pairs(state_wkv[l]), att_s, lp, None, l, depth, kv_s)
        for lst, val in zip(outs, (_from_pairs(sp), shp, _from_pairs(ss), shs)):
            lst.append(val)

    wkv_p, shift_p, wkv_s, shift_s = (jnp.stack(lst) for lst in outs)
    per_token = lambda u, b, t: u.reshape(depth, b, t, DF_HEADS, DF_VD)
    return (xp, xs, per_token(kv_p[0], bp, tp), per_token(kv_p[1], bp, tp), wkv_p, shift_p,
            per_token(kv_s[0], bs, ts), per_token(kv_s[1], bs, ts), wkv_s, shift_s)
```

```python
import functools
import math

import jax
import jax.numpy as jnp
from jax import lax
from jax.experimental import pallas as pl
from jax.experimental.pallas import tpu as pltpu

F32 = jnp.float32
BF16 = jnp.bfloat16

D_MODEL = 1024
CHUNK = 64
RW_HEADS = 8
RW_HD = 64
RW_W = RW_HEADS * RW_HD
DECAY_RANK = 64
AAA_RANK = 64
GATE_RANK = 128
RW_COLS = 3 * RW_W + DECAY_RANK + AAA_RANK + GATE_RANK
DF_HEADS = 4
DF_HD = 64
DF_VD = 2 * DF_HD
DF_QK = DF_HEADS * 2 * DF_HD
DF_V = DF_HEADS * DF_VD
P_IN = RW_COLS + 2 * DF_QK + DF_V
D_FF = 2816
RMS_EPS = 1e-6
LNX_EPS = 64e-5
SUBLN_EPS = 1e-5
QK_SCALE = DF_HD ** -0.5

V7X_LANES = 128
V7X_MXU_COLS = 256
ATTN_BLOCK = 512
POS_SPLIT = 256
ONES_ROWS = 16
ATTN_TRIP = 4
RW_CHUNK = 64
RW_TILE = 512
RW_SUB = 128
RW_PAIR = 2 * RW_HD
N_PAIRS = RW_HEADS // 2
NEG_BIG = -1e30

INV_PASSES = 1
CHUNK_PASSES = 1
STATE_PASSES = 1
SEG_PASSES = 1
CUM_PIECES = 2

VPU_STAGE, MXU_STAGE = "vpu", "mxu"

NN = (((1,), (0,)), ((), ()))
NT = (((1,), (1,)), ((), ()))
TN = (((0,), (0,)), ((), ()))


def _vmem_limit(nbytes):
    return pltpu.CompilerParams(vmem_limit_bytes=int(nbytes))


def _row_tile(n, target):
    t = min(n, target)
    while n % t:
        t -= 8
    return t


def _const_spec(shape):
    zeros = (0,) * len(shape)
    return pl.BlockSpec(shape, lambda *_: zeros, pipeline_mode=pl.Buffered(1))


def _split(x, parts):
    out, rem = [], x
    for i in range(parts):
        hi = rem.astype(BF16)
        out.append(hi)
        if i + 1 < parts:
            rem = rem - hi.astype(F32)
    return out


def _mm(a, b, dims=NN, passes=1):
    dot = functools.partial(lax.dot_general, dimension_numbers=dims, preferred_element_type=F32)
    if passes == 1:
        return dot(a.astype(BF16), b.astype(BF16))
    ah, al = _split(a, 2)
    if passes == 2:
        bh = b.astype(BF16)
        return dot(ah, bh) + dot(al, bh)
    bh, bl = _split(b, 2)
    return dot(ah, bh) + dot(al, bh) + dot(ah, bl)


def _rms(x, g, eps):
    return x * lax.rsqrt(jnp.mean(x * x, axis=-1, keepdims=True) + eps) * g


def _inproj_kernel(x_ref, g_ref, w_ref, *refs, vt_block):
    zr_ref, q_ref, k_ref, v_ref, kb_ref, vb_ref = refs[-6:]
    h = _rms(x_ref[...], g_ref[...], RMS_EPS).astype(BF16)
    step = V7X_MXU_COLS

    def proj(c0):
        return jnp.dot(h, w_ref[:, c0:c0 + step], preferred_element_type=F32)

    for c in range(0, RW_COLS, step):
        zr_ref[:, c:c + step] = proj(c)
    for c in range(0, DF_QK, step):
        q_ref[:, c:c + step] = (proj(RW_COLS + c) * QK_SCALE).astype(q_ref.dtype)
        kc = proj(RW_COLS + DF_QK + c)
        for hh in range(step // DF_VD):
            k_ref[:, c // DF_VD + hh, :] = kc[:, hh * DF_VD:(hh + 1) * DF_VD]
        kb_ref[:, c:c + step] = kc.astype(kb_ref.dtype)
        vc = proj(RW_COLS + 2 * DF_QK + c)
        for hh in range(step // DF_VD):
            v_ref[:, c // DF_VD + hh, :] = vc[:, hh * DF_VD:(hh + 1) * DF_VD]
        if vt_block is None:
            vb_ref[:, c:c + step] = vc.astype(vb_ref.dtype)
        else:
            for rb in range(vb_ref.shape[0]):
                vb_ref[rb, c:c + step, :] = vc[rb * vt_block:(rb + 1) * vt_block, :].T.astype(vb_ref.dtype)


def _inproj(x2d, g, w_bf, vt_block, layer, depth, kv_all):
    n = x2d.shape[0]
    tm = _row_tile(n, 512)
    row = lambda w: pl.BlockSpec((tm, w), lambda i: (i, 0))
    heads = lambda: pl.BlockSpec((None, tm, DF_HEADS, DF_VD), lambda i: (layer, i, 0, 0))
    carried = [] if kv_all is None else list(kv_all)
    if vt_block is None:
        vb_spec, vb_shape = row(DF_V), (n, DF_V)
    else:
        assert tm % vt_block == 0
        vb_spec = pl.BlockSpec((tm // vt_block, DF_V, vt_block), lambda i: (i, 0, 0))
        vb_shape = (n // vt_block, DF_V, vt_block)
    return pl.pallas_call(
        functools.partial(_inproj_kernel, vt_block=vt_block),
        grid=(n // tm,),
        in_specs=[row(D_MODEL), _const_spec((1, D_MODEL)), _const_spec((D_MODEL, P_IN))]
        + [pl.BlockSpec(memory_space=pl.ANY) for _ in carried],
        out_specs=[row(RW_COLS), row(DF_QK), heads(), heads(), row(DF_QK), vb_spec],
        out_shape=[
            jax.ShapeDtypeStruct((n, RW_COLS), F32),
            jax.ShapeDtypeStruct((n, DF_QK), BF16),
            jax.ShapeDtypeStruct((depth, n, DF_HEADS, 2 * DF_HD), F32),
            jax.ShapeDtypeStruct((depth, n, DF_HEADS, DF_VD), F32),
            jax.ShapeDtypeStruct((n, DF_QK), BF16),
            jax.ShapeDtypeStruct(vb_shape, BF16),
        ],
        input_output_aliases={3: 2, 4: 3} if carried else {},
        compiler_params=_vmem_limit(48 << 20),
        name="inproj",
    )(x2d, g.reshape(1, D_MODEL), w_bf, *carried)


def _neumann_inverse(dab, passes):
    n = dab[0].shape[0]
    ii = lax.broadcasted_iota(jnp.int32, (n, n), 0)
    jj = lax.broadcasted_iota(jnp.int32, (n, n), 1)
    same16 = (ii >> 4) == (jj >> 4)
    same32 = (ii >> 5) == (jj >> 5)
    eye = jnp.where(ii == jj, 1.0, 0.0)
    mm = functools.partial(_mm, passes=passes)
    x = [jnp.where(same16, d, 0.0) for d in dab]
    p = [eye + xi for xi in x]
    x = [mm(xi, xi) for xi in x]
    yield MXU_STAGE
    for _ in range(2):
        xp = [mm(xi, jnp.concatenate([xi, pi], axis=1)) for xi, pi in zip(x, p)]
        p = [pi + r[:, n:] for pi, r in zip(p, xp)]
        x = [r[:, :n] for r in xp]
        yield MXU_STAGE
    p = [pi + mm(xi, pi) for xi, pi in zip(x, p)]
    yield MXU_STAGE
    for off_mask in (same32 & jnp.logical_not(same16), jnp.logical_not(same32)):
        off = [jnp.where(off_mask, d, 0.0) for d in dab]
        po = [mm(pi, oi) for pi, oi in zip(p, off)]
        yield MXU_STAGE
        p = [pi + mm(qi, pi) for pi, qi in zip(p, po)]
        yield MXU_STAGE
    return p


def _rwkv_rows(zr_ref, mu_ref, w0_ref, a0_ref, w2a2_ref, g2_ref, kk_ref, ka_ref, rk_ref, lnw_ref, lnb_ref, seg_ref,
               y_ref, s_s, prev_s, r0, tt, t_valid, order):
    zr = zr_ref[0, r0:r0 + tt, :]
    row = lax.broadcasted_iota(jnp.int32, (tt, 1), 0)
    before = prev_s[0:1, :] if r0 == 0 else zr_ref[0, r0 - 1:r0, :]
    zprev = jnp.where(row == 0, before, pltpu.roll(zr, 1, axis=0))
    zm = zr + (zprev - zr) * mu_ref[...]
    yield VPU_STAGE
    r = zm[:, 0:RW_W]
    k = zm[:, RW_W:2 * RW_W]
    v = zm[:, 2 * RW_W:3 * RW_W]
    xwa = zm[:, 3 * RW_W:3 * RW_W + DECAY_RANK + AAA_RANK]
    gd = zm[:, 3 * RW_W + DECAY_RANK + AAA_RANK:RW_COLS]
    lane = lax.broadcasted_iota(jnp.int32, xwa.shape, 1)
    xwa = jnp.where(lane < DECAY_RANK, jnp.tanh(xwa), xwa)
    lora = jnp.dot(xwa.astype(BF16), w2a2_ref[...], preferred_element_type=F32)
    yield VPU_STAGE
    lw = -math.exp(-0.5) * jax.nn.sigmoid(w0_ref[...] + lora[:, 0:RW_W])
    a = jax.nn.sigmoid(a0_ref[...] + lora[:, RW_W:2 * RW_W])
    gate = jnp.dot(jax.nn.sigmoid(gd).astype(BF16), g2_ref[...], preferred_element_type=F32)
    yield VPU_STAGE

    def segsum(x):
        return sum(jnp.dot(part, seg_ref[...], preferred_element_type=F32) for part in _split(x, SEG_PASSES))

    kkn = k * kk_ref[...]
    kkn = kkn * lax.rsqrt(jnp.maximum(segsum(kkn * kkn), 1e-24))
    yield VPU_STAGE
    k2 = k * (1.0 + (a - 1.0) * ka_ref[...])
    if t_valid < tt:
        valid = row < t_valid
        lw = jnp.where(valid, lw, 0.0)
        kkn = jnp.where(valid, kkn, 0.0)
        k2 = jnp.where(valid, k2, 0.0)
        v = jnp.where(valid, v, 0.0)

    c = RW_CHUNK
    ti = lax.broadcasted_iota(jnp.int32, (tt, tt), 0)
    tj = lax.broadcasted_iota(jnp.int32, (tt, tt), 1)
    cum = jnp.where(((ti >> 6) == (tj >> 6)) & (ti >= tj), 1.0, 0.0).astype(BF16)
    g = sum(jnp.dot(cum, part, preferred_element_type=F32) for part in _split(lw, CUM_PIECES))
    gc = jnp.concatenate([jnp.broadcast_to(g[e - 1:e, :], (c, RW_W)) for e in range(c, tt + 1, c)], axis=0)
    yield VPU_STAGE
    ieg = jnp.exp(-g)
    egc = jnp.exp(gc - g)
    b = kkn * a
    at = -kkn * jnp.exp(g - lw)
    rt = r * jnp.exp(g)
    yield VPU_STAGE
    bh, kh = b * ieg, k2 * ieg
    bb, kb = b * egc, k2 * egc
    gam = jnp.exp(gc)
    yield VPU_STAGE

    probs = [(ci, p) for ci in range(tt // c) for p in range(N_PAIRS)]

    def blk(x, ci, p):
        return x[ci * c:(ci + 1) * c, p * RW_PAIR:(p + 1) * RW_PAIR]

    head0 = lax.broadcasted_iota(jnp.int32, (c, RW_PAIR), 1) < RW_HD

    def stack(x):
        return jnp.concatenate([jnp.where(head0, x, 0.0), jnp.where(head0, 0.0, x)], axis=0)

    def twice(x):
        return jnp.concatenate([x, x], axis=0)

    n = 2 * c
    ii = lax.broadcasted_iota(jnp.int32, (n, n), 0)
    jj = lax.broadcasted_iota(jnp.int32, (n, n), 1)
    same_head = (ii >> 6) == (jj >> 6)
    strict = same_head & (ii > jj)
    incl = same_head & (ii >= jj)

    ats = [stack(blk(at, *q)) for q in probs]
    rts = [stack(blk(rt, *q)) for q in probs]
    vs = [stack(blk(v, *q)) for q in probs]
    aa = [_mm(jnp.concatenate([a_, r_], axis=0),
              jnp.concatenate([twice(blk(bh, *q)), twice(blk(kh, *q))], axis=0), NT, CHUNK_PASSES)
          for a_, r_, q in zip(ats, rts, probs)]
    yield MXU_STAGE
    dab = [jnp.where(strict, x[0:n, 0:n], 0.0) for x in aa]
    dak = [jnp.where(strict, x[0:n, n:2 * n], 0.0) for x in aa]
    drbk = [jnp.concatenate([jnp.where(incl, x[n:2 * n, 0:n], 0.0),
                             jnp.where(incl, x[n:2 * n, n:2 * n], 0.0)], axis=1) for x in aa]
    dv = [_mm(d, v_, NN, CHUNK_PASSES) for d, v_ in zip(dak, vs)]
    tinv = yield from _neumann_inverse(dab, INV_PASSES)
    gh = [_mm(t_, jnp.concatenate([a_, d], axis=1), NN, INV_PASSES) for t_, a_, d in zip(tinv, ats, dv)]
    yield MXU_STAGE
    zeros = jnp.zeros((n, n), F32)
    qy = [_mm(d, jnp.concatenate([x, jnp.concatenate([zeros, v_], axis=1)], axis=0), NN, CHUNK_PASSES)
          for d, x, v_ in zip(drbk, gh, vs)]
    yield MXU_STAGE
    qe = [r_ + x[:, 0:n] for r_, x in zip(rts, qy)]
    y0 = [x[:, n:2 * n] for x in qy]
    bbs = [stack(blk(bb, *q)) for q in probs]
    kbs = [stack(blk(kb, *q)) for q in probs]
    mz = [_mm(jnp.concatenate([x, jnp.concatenate([zeros, v_], axis=1)], axis=0),
              jnp.concatenate([b_, k_], axis=0), TN, STATE_PASSES)
          for x, v_, b_, k_ in zip(gh, vs, bbs, kbs)]
    yield MXU_STAGE
    eye = ii == jj
    mx = [jnp.where(eye, blk(gam, *q)[0:1, :], 0.0) + x[0:n] for x, q in zip(mz, probs)]
    zz = [x[n:2 * n] for x in mz]

    assert order[0] == r0, "sub-tiles must reach their state update in time order"
    state = [s_s[p] for p in range(N_PAIRS)]
    y_rows = []
    for ci in range(tt // c):
        y_pairs = []
        for p in range(N_PAIRS):
            i = ci * N_PAIRS + p
            ys = _mm(qe[i], state[p], NT, STATE_PASSES) + y0[i]
            y_pairs.append(ys[0:c] + ys[c:n])
            state[p] = _mm(state[p], mx[i], NN, STATE_PASSES) + zz[i]
        y_rows.append(jnp.concatenate(y_pairs, axis=1))
    for p in range(N_PAIRS):
        s_s[p] = state[p]
    order[0] = r0 + tt
    yield MXU_STAGE
    y = jnp.concatenate(y_rows, axis=0)

    inv_hd = 1.0 / RW_HD
    d = y - segsum(y) * inv_hd
    yield VPU_STAGE
    yn = d * lax.rsqrt(segsum(d * d) * inv_hd + LNX_EPS) * lnw_ref[...] + lnb_ref[...]
    yn = yn + segsum(r * k2 * rk_ref[...]) * v
    y_ref[0, r0:r0 + tt, :] = (yn * gate).astype(y_ref.dtype)


def _rwkv_kernel(zr_ref, shift_ref, s0_ref, mu_ref, w0_ref, a0_ref, w2a2_ref, g2_ref, kk_ref, ka_ref,
                 rk_ref, lnw_ref, lnb_ref, seg_ref, y_ref, sfin_ref, s_s, prev_s, *, tt, sub, t_valid):
    t = pl.program_id(1)

    @pl.when(t == 0)
    def _():
        s_s[...] = s0_ref[0]
        prev_s[0:1, :] = shift_ref[0]

    order = [0]
    pending = [_rwkv_rows(zr_ref, mu_ref, w0_ref, a0_ref, w2a2_ref, g2_ref, kk_ref, ka_ref, rk_ref, lnw_ref,
                          lnb_ref, seg_ref, y_ref, s_s, prev_s, r0, sub, t_valid, order)
               for r0 in range(0, tt, sub)]
    active = []
    while pending or active:
        if pending and all(stage == MXU_STAGE for _, stage in active):
            active.append([pending.pop(0), VPU_STAGE])
        for item in list(active):
            try:
                item[1] = next(item[0])
            except StopIteration:
                active.remove(item)
    prev_s[0:1, :] = zr_ref[0, tt - 1:tt, :]

    @pl.when(t == pl.num_programs(1) - 1)
    def _():
        sfin_ref[0] = s_s[...]


def _rwkv(zr, shift0, s0_pairs, lp, t_valid):
    b, tp, _ = zr.shape
    tt = _row_tile(tp, RW_TILE)
    assert t_valid == tp or tp == tt, "padding is only supported inside a single time tile"
    vec = lambda w: _const_spec((1, w))
    return pl.pallas_call(
        functools.partial(_rwkv_kernel, tt=tt, sub=min(tt, RW_SUB), t_valid=tt if t_valid == tp else t_valid),
        grid=(b, tp // tt),
        in_specs=[
            pl.BlockSpec((1, tt, RW_COLS), lambda i, t: (i, t, 0)),
            pl.BlockSpec((1, 1, RW_COLS), lambda i, t: (i, 0, 0)),
            pl.BlockSpec((1, N_PAIRS, RW_PAIR, RW_PAIR), lambda i, t: (i, 0, 0, 0)),
            vec(RW_COLS), vec(RW_W), vec(RW_W),
            _const_spec((DECAY_RANK + AAA_RANK, 2 * RW_W)), _const_spec((GATE_RANK, RW_W)),
            vec(RW_W), vec(RW_W), vec(RW_W), vec(RW_W), vec(RW_W),
            _const_spec((RW_W, RW_W)),
        ],
        out_specs=[
            pl.BlockSpec((1, tt, RW_W), lambda i, t: (i, t, 0)),
            pl.BlockSpec((1, N_PAIRS, RW_PAIR, RW_PAIR), lambda i, t: (i, 0, 0, 0)),
        ],
        out_shape=[
            jax.ShapeDtypeStruct((b, tp, RW_W), BF16),
            jax.ShapeDtypeStruct((b, N_PAIRS, RW_PAIR, RW_PAIR), F32),
        ],
        scratch_shapes=[
            pltpu.VMEM((N_PAIRS, RW_PAIR, RW_PAIR), F32),
            pltpu.VMEM((8, RW_COLS), F32),
        ],
        compiler_params=pltpu.CompilerParams(
            dimension_semantics=("parallel", "arbitrary"), vmem_limit_bytes=48 << 20),
        name="rwkv",
    )(zr, shift0, s0_pairs, lp["mu"], lp["w0"], lp["a0"], lp["w2a2"], lp["g2"], lp["k_k"], lp["k_a"],
      lp["r_k"], lp["lnx_w"], lp["lnx_b"], lp["seg"])


def _to_pairs(s):
    b = s.shape[0]
    s = s.reshape(b, N_PAIRS, 2, RW_HD, RW_HD)
    eye = jnp.eye(2, dtype=s.dtype)
    return jnp.einsum("bphvk,hg->bphvgk", s, eye).reshape(b, N_PAIRS, RW_PAIR, RW_PAIR)


def _from_pairs(sp):
    b = sp.shape[0]
    sp = sp.reshape(b, N_PAIRS, 2, RW_HD, 2, RW_HD)
    return jnp.stack([sp[:, :, 0, :, 0, :], sp[:, :, 1, :, 1, :]], axis=2).reshape(b, RW_HEADS, RW_HD, RW_HD)


def _stack_maps(q):
    lane = lax.broadcasted_iota(jnp.int32, q.shape, 1)
    zero = jnp.zeros_like(q)
    return jnp.concatenate([jnp.where(lane < DF_HD, q, zero), jnp.where(lane < DF_HD, zero, q)], axis=0)


def _attn_prompt_kernel(sc_ref, q_ref, k_ref, kpos_ref, vt_ref, g_ref, dbias_ref, o_ref, m_s, acc_s, s_s, smax_s,
                        *, tb):
    h = pl.program_id(1)
    i = pl.program_id(2)
    lam, out_scale, slope = sc_ref[0], sc_ref[1], sc_ref[2 + h]
    qt = q_ref[0].astype(F32).T
    dim = lax.broadcasted_iota(jnp.int32, (DF_VD, tb), 0)
    q_pos = jnp.where(dim == 0, slope * POS_SPLIT, jnp.where(dim == 1, slope, 0.0))
    qs_t = jnp.concatenate(
        [jnp.concatenate([jnp.where(dim < DF_HD, qt, 0.0), jnp.where(dim < DF_HD, 0.0, qt)], axis=1),
         jnp.concatenate([q_pos, q_pos], axis=1)], axis=0).astype(BF16)

    def scores(j):
        keys = jnp.concatenate([k_ref[0, j], kpos_ref[j]], axis=1)
        return jnp.dot(keys, qs_t, preferred_element_type=F32)

    ones = jnp.ones((ONES_ROWS, tb), BF16)

    def weighted_values(j, p):
        return jnp.dot(jnp.concatenate([vt_ref[0, j], ones], axis=0), p, preferred_element_type=F32)

    def stash_scores(j):
        sc = scores(j)
        s_s[...] = sc
        smax_s[...] = jnp.max(sc, axis=0, keepdims=True)

    stash_scores(0)
    s = scores(i) + dbias_ref[0]
    m = jnp.max(s, axis=0, keepdims=True)
    m_s[...] = m
    acc_s[...] = weighted_values(i, jnp.exp(s - m).astype(BF16))

    def step(j, stash_next):
        s = s_s[...]
        m_old = m_s[...]
        m_new = jnp.maximum(m_old, smax_s[...])
        if stash_next:
            stash_scores(j + 1)
        p = jnp.exp(s - m_new).astype(BF16)
        acc_s[...] = jnp.exp(m_old - m_new) * acc_s[...] + weighted_values(j, p)
        m_s[...] = m_new

    n_lead = jnp.maximum(i - 1, 0)
    n_trips = n_lead // ATTN_TRIP

    def trip(jj, carry):
        for u in range(ATTN_TRIP):
            step(ATTN_TRIP * jj + u, True)
        return carry

    lax.fori_loop(0, n_trips, trip, 0)
    for u in range(ATTN_TRIP - 1):
        @pl.when(n_lead - ATTN_TRIP * n_trips > u)
        def _():
            step(ATTN_TRIP * n_trips + u, True)

    @pl.when(i > 0)
    def _():
        step(i - 1, False)

    acc = acc_s[...]
    on = acc[0:DF_VD] / acc[DF_VD:DF_VD + 1]
    ot = on[:, 0:tb] - lam * on[:, tb:2 * tb]
    ot = ot * lax.rsqrt(jnp.mean(ot * ot, axis=0, keepdims=True) + SUBLN_EPS)
    o_ref[0] = (ot.T * g_ref[...] * out_scale).astype(o_ref.dtype)


def _attn_prompt(scal, q, k, vt, g, tb):
    b, t, _ = q.shape
    nb = t // tb
    pos = jnp.arange(t, dtype=jnp.int32).reshape(nb, tb, 1)
    lane = jnp.arange(V7X_LANES, dtype=jnp.int32)
    kpos = jnp.where(lane == 0, pos // POS_SPLIT, jnp.where(lane == 1, pos % POS_SPLIT, 0)).astype(BF16)
    kl = jnp.arange(tb, dtype=jnp.int32)[:, None]
    ql = jnp.arange(2 * tb, dtype=jnp.int32)[None, :] % tb
    after = jnp.maximum(kl - ql, 0).astype(F32)
    dbias = jnp.where((kl // CHUNK) <= (ql // CHUNK), -2.0 * scal[2:2 + DF_HEADS, None, None] * after, NEG_BIG)
    blk = lambda: pl.BlockSpec((1, tb, DF_VD), lambda bi, h, i: (bi, i, h))
    return pl.pallas_call(
        functools.partial(_attn_prompt_kernel, tb=tb),
        grid=(b, DF_HEADS, nb),
        in_specs=[pl.BlockSpec(memory_space=pltpu.SMEM), blk(),
                  pl.BlockSpec((1, nb, tb, DF_VD), lambda bi, h, i: (bi, 0, 0, h)),
                  _const_spec((nb, tb, V7X_LANES)),
                  pl.BlockSpec((1, nb, DF_VD, tb), lambda bi, h, i: (bi, 0, h, 0)),
                  _const_spec((1, DF_VD)),
                  pl.BlockSpec((1, tb, 2 * tb), lambda bi, h, i: (h, 0, 0))],
        out_specs=blk(),
        out_shape=jax.ShapeDtypeStruct((b, t, DF_V), BF16),
        scratch_shapes=[pltpu.VMEM((1, 2 * tb), F32),
                        pltpu.VMEM((DF_VD + ONES_ROWS, 2 * tb), F32),
                        pltpu.VMEM((tb, 2 * tb), F32),
                        pltpu.VMEM((1, 2 * tb), F32)],
        compiler_params=pltpu.CompilerParams(
            dimension_semantics=("parallel", "parallel", "arbitrary"), vmem_limit_bytes=32 << 20),
        name="attn_prompt",
    )(scal, q, k, kpos, vt, g, dbias)


def _attn_sample_kernel(sc_ref, q_ref, kn_ref, vn_ref, kp_ref, vp_ref, g_ref, o_ref, *, t, past):
    lam, out_scale = sc_ref[0], sc_ref[1]
    for h in range(DF_HEADS):
        slope = sc_ref[2 + h]
        lanes = slice(h * DF_VD, (h + 1) * DF_VD)
        qs = _stack_maps(q_ref[0, :, lanes])

        def scores(keys, k0):
            nk = keys.shape[0]
            s = lax.dot_general(qs, keys.astype(BF16), NT, preferred_element_type=F32)
            qpos = past + lax.rem(lax.broadcasted_iota(jnp.int32, (2 * t, nk), 0), t)
            kpos = k0 + lax.broadcasted_iota(jnp.int32, (2 * t, nk), 1)
            s = s - slope * jnp.abs(qpos - kpos).astype(F32)
            return jnp.where((kpos >> 6) <= (qpos >> 6), s, NEG_BIG)

        sp = scores(kp_ref[0, :, h, :], 0)
        sn = scores(kn_ref[0, :, lanes], past)
        m = jnp.maximum(jnp.max(sp, axis=-1, keepdims=True), jnp.max(sn, axis=-1, keepdims=True))
        pp = jnp.exp(sp - m)
        pn = jnp.exp(sn - m)
        l = jnp.sum(pp, axis=-1, keepdims=True) + jnp.sum(pn, axis=-1, keepdims=True)
        acc = (jnp.dot(pp.astype(BF16), vp_ref[0, :, h, :].astype(BF16), preferred_element_type=F32)
               + jnp.dot(pn.astype(BF16), vn_ref[0, :, lanes].astype(BF16), preferred_element_type=F32))
        on = acc / l
        o = on[0:t] - lam * on[t:2 * t]
        o_ref[0, :, lanes] = (_rms(o, g_ref[...], SUBLN_EPS) * out_scale).astype(o_ref.dtype)


def _attn_sample(scal, q, k, v, cache_k, cache_v, layer, g):
    b, t, _ = q.shape
    past = cache_k.shape[2]
    new = lambda: pl.BlockSpec((1, t, DF_V), lambda bi: (bi, 0, 0))
    old = lambda: pl.BlockSpec((None, 1, past, DF_HEADS, DF_VD), lambda bi: (layer, bi, 0, 0, 0))
    return pl.pallas_call(
        functools.partial(_attn_sample_kernel, t=t, past=past),
        grid=(b,),
        in_specs=[pl.BlockSpec(memory_space=pltpu.SMEM), new(), new(), new(), old(), old(),
                  _const_spec((1, DF_VD))],
        out_specs=new(),
        out_shape=jax.ShapeDtypeStruct((b, t, DF_V), BF16),
        compiler_params=pltpu.CompilerParams(dimension_semantics=("parallel",), vmem_limit_bytes=48 << 20),
        name="attn_sample",
    )(scal, q, k, v, cache_k, cache_v, g)


FF_STEP = 256


def _mlp_kernel(x_ref, y_ref, o_ref, wo_ref, g_ref, wg_ref, wu_ref, wd_ref, gf_ref, out_ref, h_s, *, final_norm):
    x1 = (x_ref[...]
          + jnp.dot(y_ref[...], wo_ref[0:RW_W, :], preferred_element_type=F32)
          + jnp.dot(o_ref[...], wo_ref[RW_W:RW_W + DF_V, :], preferred_element_type=F32))
    out_ref[...] = x1
    h_s[...] = _rms(x1, g_ref[...], RMS_EPS).astype(BF16)
    for c in range(0, D_FF, FF_STEP):
        h2 = h_s[...]
        gt = jnp.dot(h2, wg_ref[:, c:c + FF_STEP], preferred_element_type=F32)
        up = jnp.dot(h2, wu_ref[:, c:c + FF_STEP], preferred_element_type=F32)
        act = (gt * jax.nn.sigmoid(gt) * up).astype(BF16)
        out_ref[...] += jnp.dot(act, wd_ref[c:c + FF_STEP, :], preferred_element_type=F32)
    if final_norm:
        out_ref[...] = _rms(out_ref[...], gf_ref[...], RMS_EPS)


def _mlp(x2d, y_rw, o_df, wo, g, wg, wu, wd, final_g, final_norm):
    n = x2d.shape[0]
    tm = _row_tile(n, 512)
    row = lambda w: pl.BlockSpec((tm, w), lambda i: (i, 0))
    return pl.pallas_call(
        functools.partial(_mlp_kernel, final_norm=final_norm),
        grid=(n // tm,),
        in_specs=[row(D_MODEL), row(RW_W), row(DF_V), _const_spec((RW_W + DF_V, D_MODEL)),
                  _const_spec((1, D_MODEL)), _const_spec((D_MODEL, D_FF)), _const_spec((D_MODEL, D_FF)),
                  _const_spec((D_FF, D_MODEL)), _const_spec((1, D_MODEL))],
        out_specs=row(D_MODEL),
        out_shape=jax.ShapeDtypeStruct((n, D_MODEL), F32),
        scratch_shapes=[pltpu.VMEM((tm, D_MODEL), BF16)],
        compiler_params=_vmem_limit(52 << 20),
        name="mlp",
    )(x2d, y_rw, o_df, wo, g.reshape(1, D_MODEL), wg, wu, wd, final_g.reshape(1, D_MODEL))


def _layer(x, shift0, s0_pairs, attend, lp, vt_block, layer, depth, kv_all):
    b, t, _ = x.shape
    n = b * t
    zr, q, k_all, v_all, kb, vb = _inproj(x.reshape(n, D_MODEL), lp["norm1"], lp["w_in"], vt_block, layer, depth,
                                          kv_all)
    zr = zr.reshape(b, t, RW_COLS)
    tp = -(-t // RW_CHUNK) * RW_CHUNK
    zr_p = zr if tp == t else jnp.pad(zr, ((0, 0), (0, tp - t), (0, 0)))
    y_rw, s_fin = _rwkv(zr_p, shift0, s0_pairs, lp, t)
    if tp != t:
        y_rw = y_rw[:, :t]
    o = attend(q.reshape(b, t, DF_QK), kb, vb)
    x_new = _mlp(x.reshape(n, D_MODEL), y_rw.reshape(n, RW_W), o.reshape(n, DF_V), lp["w_out"], lp["norm2"],
                 lp["wg"], lp["wu"], lp["wd"], lp["final_g"], layer == depth - 1)
    return x_new.reshape(b, t, D_MODEL), zr[:, t - 1:t], s_fin, (k_all, v_all)


@jax.jit
def kernel(x_prompt, x_sample, cache_k, cache_v, state_wkv, state_shift, norm1_g, w_in, rw_mu, rw_w0, rw_w2,
           rw_a0, rw_a2, rw_g2, rw_k_k, rw_k_a, rw_r_k, rw_lnx_w, rw_lnx_b, df_lq1, df_lk1, df_lq2, df_lk2,
           df_subln_g, w_out, norm2_g, ffn_w_gate, ffn_w_up, ffn_w_down, final_g):
    depth = w_in.shape[0]
    bp, tp = x_prompt.shape[0], x_prompt.shape[1]
    bs, ts = x_sample.shape[0], x_sample.shape[1]
    past = cache_k.shape[2]
    tb = _row_tile(tp, ATTN_BLOCK)
    assert tb & (tb - 1) == 0 and tb % CHUNK == 0
    slopes = 2.0 ** (-8.0 * jnp.arange(1, DF_HEADS + 1, dtype=F32) / DF_HEADS)
    head = jnp.arange(RW_W) // RW_HD
    seg = (head[:, None] == head[None, :]).astype(BF16)
    zero_blk = jnp.zeros((DECAY_RANK, RW_W), F32)
    w_in_b, w_out_b = w_in.astype(BF16), w_out.astype(BF16)
    wg_b, wu_b, wd_b = ffn_w_gate.astype(BF16), ffn_w_up.astype(BF16), ffn_w_down.astype(BF16)

    xp, xs = x_prompt, x_sample
    kv_p = kv_s = None
    outs = [[] for _ in range(4)]
    for l in range(depth):
        lam_init = 0.8 - 0.6 * math.exp(-0.3 * l)
        lam = (jnp.exp(jnp.sum((df_lq1[l] * df_lk1[l]).astype(F32)))
               - jnp.exp(jnp.sum((df_lq2[l] * df_lk2[l]).astype(F32))) + lam_init)
        scal = jnp.concatenate([jnp.stack([lam, jnp.asarray(1.0 - lam_init, F32)]), slopes]).astype(F32)
        row = lambda u: u[l].reshape(1, -1)
        lp = dict(
            norm1=norm1_g[l], w_in=w_in_b[l], mu=row(rw_mu), w0=row(rw_w0), a0=row(rw_a0),
            w2a2=jnp.concatenate([jnp.concatenate([rw_w2[l], zero_blk], axis=1),
                                  jnp.concatenate([zero_blk, rw_a2[l]], axis=1)], axis=0).astype(BF16),
            g2=rw_g2[l].astype(BF16), k_k=row(rw_k_k), k_a=row(rw_k_a), r_k=row(rw_r_k),
            lnx_w=row(rw_lnx_w), lnx_b=row(rw_lnx_b), seg=seg, w_out=w_out_b[l], norm2=norm2_g[l],
            wg=wg_b[l], wu=wu_b[l], wd=wd_b[l], final_g=final_g)
        g_sub = df_subln_g[l].reshape(1, DF_VD)

        att_p = lambda q, kb, vt: _attn_prompt(scal, q, kb.reshape(bp, tp // tb, tb, DF_QK),
                                               vt.reshape(bp, tp // tb, DF_V, tb), g_sub, tb)
        xp, shp, sp, kv_p = _layer(xp, jnp.zeros((bp, 1, RW_COLS), F32),
                                   jnp.zeros((bp, N_PAIRS, RW_PAIR, RW_PAIR), F32), att_p, lp, tb, l, depth, kv_p)
        att_s = lambda q, kb, vb: _attn_sample(scal, q, kb.reshape(bs, ts, DF_QK), vb.reshape(bs, ts, DF_V),
                                               cache_k, cache_v, l, g_sub)
        xs, shs, ss, kv_s = _layer(xs, state_shift[l], _to_pairs(state_wkv[l]), att_s, lp, None, l, depth, kv_s)
        for lst, val in zip(outs, (_from_pairs(sp), shp, _from_pairs(ss), shs)):
            lst.append(val)

    wkv_p, shift_p, wkv_s, shift_s = (jnp.stack(lst) for lst in outs)
    per_token = lambda u, b, t: u.reshape(depth, b, t, DF_HEADS, DF_VD)
    return (xp, xs, per_token(kv_p[0], bp, tp), per_token(kv_p[1], bp, tp), wkv_p, shift_p,
            per_token(kv_s[0], bs, ts), per_token(kv_s[1], bs, ts), wkv_s, shift_s)
```

```python
import functools
import math

import jax
import jax.numpy as jnp
from jax import lax
from jax.experimental import pallas as pl
from jax.experimental.pallas import tpu as pltpu

F32 = jnp.float32
BF16 = jnp.bfloat16

D_MODEL = 1024
CHUNK = 64
RW_HEADS = 8
RW_HD = 64
RW_W = RW_HEADS * RW_HD
DECAY_RANK = 64
AAA_RANK = 64
GATE_RANK = 128
RW_COLS = 3 * RW_W + DECAY_RANK + AAA_RANK + GATE_RANK
DF_HEADS = 4
DF_HD = 64
DF_VD = 2 * DF_HD
DF_QK = DF_HEADS * 2 * DF_HD
DF_V = DF_HEADS * DF_VD
P_IN = RW_COLS + 2 * DF_QK + DF_V
D_FF = 2816
RMS_EPS = 1e-6
LNX_EPS = 64e-5
SUBLN_EPS = 1e-5
QK_SCALE = DF_HD ** -0.5

V7X_LANES = 128
V7X_MXU_COLS = 256
ATTN_BLOCK = 512
POS_SPLIT = 256
ONES_ROWS = 16
ATTN_TRIP = 4
RW_CHUNK = 64
RW_TILE = 512
RW_SUB = 128
RW_PAIR = 2 * RW_HD
N_PAIRS = RW_HEADS // 2
NEG_BIG = -1e30

INV_PASSES = 1
CHUNK_PASSES = 1
STATE_PASSES = 1
SEG_PASSES = 1
CUM_PIECES = 2

VPU_STAGE, MXU_STAGE = "vpu", "mxu"

NN = (((1,), (0,)), ((), ()))
NT = (((1,), (1,)), ((), ()))
TN = (((0,), (0,)), ((), ()))


def _vmem_limit(nbytes):
    return pltpu.CompilerParams(vmem_limit_bytes=int(nbytes))


def _row_tile(n, target):
    t = min(n, target)
    while n % t:
        t -= 8
    return t


def _const_spec(shape):
    zeros = (0,) * len(shape)
    return pl.BlockSpec(shape, lambda *_: zeros, pipeline_mode=pl.Buffered(1))


def _split(x, parts):
    out, rem = [], x
    for i in range(parts):
        hi = rem.astype(BF16)
        out.append(hi)
        if i + 1 < parts:
            rem = rem - hi.astype(F32)
    return out


def _mm(a, b, dims=NN, passes=1):
    dot = functools.partial(lax.dot_general, dimension_numbers=dims, preferred_element_type=F32)
    if passes == 1:
        return dot(a.astype(BF16), b.astype(BF16))
    ah, al = _split(a, 2)
    if passes == 2:
        bh = b.astype(BF16)
        return dot(ah, bh) + dot(al, bh)
    bh, bl = _split(b, 2)
    return dot(ah, bh) + dot(al, bh) + dot(ah, bl)


def _rms(x, g, eps):
    return x * lax.rsqrt(jnp.mean(x * x, axis=-1, keepdims=True) + eps) * g


def _inproj_kernel(x_ref, g_ref, w_ref, k_all_ref, v_all_ref, zr_ref, q_ref, k_ref, v_ref, kb_ref, vb_ref, *, vt_block):
    del k_all_ref, v_all_ref
    h = _rms(x_ref[...], g_ref[...], RMS_EPS).astype(BF16)
    step = V7X_MXU_COLS

    def proj(c0):
        return jnp.dot(h, w_ref[:, c0:c0 + step], preferred_element_type=F32)

    for c in range(0, RW_COLS, step):
        zr_ref[:, c:c + step] = proj(c)
    for c in range(0, DF_QK, step):
        q_ref[:, c:c + step] = (proj(RW_COLS + c) * QK_SCALE).astype(q_ref.dtype)
        kc = proj(RW_COLS + DF_QK + c)
        for hh in range(step // DF_VD):
            k_ref[:, c // DF_VD + hh, :] = kc[:, hh * DF_VD:(hh + 1) * DF_VD]
        kb_ref[:, c:c + step] = kc.astype(kb_ref.dtype)
        vc = proj(RW_COLS + 2 * DF_QK + c)
        for hh in range(step // DF_VD):
            v_ref[:, c // DF_VD + hh, :] = vc[:, hh * DF_VD:(hh + 1) * DF_VD]
        if vt_block is None:
            vb_ref[:, c:c + step] = vc.astype(vb_ref.dtype)
        else:
            for rb in range(vb_ref.shape[0]):
                vb_ref[rb, c:c + step, :] = vc[rb * vt_block:(rb + 1) * vt_block, :].T.astype(vb_ref.dtype)


def _inproj(x2d, g, w_bf, vt_block, layer, depth, kv_all):
    n = x2d.shape[0]
    tm = _row_tile(n, 512)
    row = lambda w: pl.BlockSpec((tm, w), lambda i: (i, 0))
    heads = lambda: pl.BlockSpec((None, tm, DF_HEADS, DF_VD), lambda i: (layer, i, 0, 0))
    if vt_block is None:
        vb_spec, vb_shape = row(DF_V), (n, DF_V)
    else:
        assert tm % vt_block == 0
        vb_spec = pl.BlockSpec((tm // vt_block, DF_V, vt_block), lambda i: (i, 0, 0))
        vb_shape = (n // vt_block, DF_V, vt_block)
    return pl.pallas_call(
        functools.partial(_inproj_kernel, vt_block=vt_block),
        grid=(n // tm,),
        in_specs=[row(D_MODEL), _const_spec((1, D_MODEL)), _const_spec((D_MODEL, P_IN))]
        + [pl.BlockSpec(memory_space=pl.ANY), pl.BlockSpec(memory_space=pl.ANY)],
        out_specs=[row(RW_COLS), row(DF_QK), heads(), heads(), row(DF_QK), vb_spec],
        out_shape=[
            jax.ShapeDtypeStruct((n, RW_COLS), F32),
            jax.ShapeDtypeStruct((n, DF_QK), BF16),
            jax.ShapeDtypeStruct((depth, n, DF_HEADS, 2 * DF_HD), F32),
            jax.ShapeDtypeStruct((depth, n, DF_HEADS, DF_VD), F32),
            jax.ShapeDtypeStruct((n, DF_QK), BF16),
            jax.ShapeDtypeStruct(vb_shape, BF16),
        ],
        input_output_aliases={3: 2, 4: 3},
        compiler_params=_vmem_limit(48 << 20),
        name="inproj",
    )(x2d, g.reshape(1, D_MODEL), w_bf, *kv_all)


def _neumann_inverse(dab, passes):
    n = dab[0].shape[0]
    ii = lax.broadcasted_iota(jnp.int32, (n, n), 0)
    jj = lax.broadcasted_iota(jnp.int32, (n, n), 1)
    same16 = (ii >> 4) == (jj >> 4)
    same32 = (ii >> 5) == (jj >> 5)
    eye = jnp.where(ii == jj, 1.0, 0.0)
    mm = functools.partial(_mm, passes=passes)
    x = [jnp.where(same16, d, 0.0) for d in dab]
    p = [eye + xi for xi in x]
    x = [mm(xi, xi) for xi in x]
    yield MXU_STAGE
    for _ in range(2):
        xp = [mm(xi, jnp.concatenate([xi, pi], axis=1)) for xi, pi in zip(x, p)]
        p = [pi + r[:, n:] for pi, r in zip(p, xp)]
        x = [r[:, :n] for r in xp]
        yield MXU_STAGE
    p = [pi + mm(xi, pi) for xi, pi in zip(x, p)]
    yield MXU_STAGE
    for off_mask in (same32 & jnp.logical_not(same16), jnp.logical_not(same32)):
        off = [jnp.where(off_mask, d, 0.0) for d in dab]
        po = [mm(pi, oi) for pi, oi in zip(p, off)]
        yield MXU_STAGE
        p = [pi + mm(qi, pi) for pi, qi in zip(p, po)]
        yield MXU_STAGE
    return p


def _rwkv_rows(zr_ref, mu_ref, w0_ref, a0_ref, w2a2_ref, g2_ref, kk_ref, ka_ref, rk_ref, lnw_ref, lnb_ref, seg_ref,
               y_ref, s_s, prev_s, r0, tt, t_valid, order):
    zr = zr_ref[0, r0:r0 + tt, :]
    row = lax.broadcasted_iota(jnp.int32, (tt, 1), 0)
    before = prev_s[0:1, :] if r0 == 0 else zr_ref[0, r0 - 1:r0, :]
    zprev = jnp.where(row == 0, before, pltpu.roll(zr, 1, axis=0))
    zm = zr + (zprev - zr) * mu_ref[...]
    yield VPU_STAGE
    r = zm[:, 0:RW_W]
    k = zm[:, RW_W:2 * RW_W]
    v = zm[:, 2 * RW_W:3 * RW_W]
    xwa = zm[:, 3 * RW_W:3 * RW_W + DECAY_RANK + AAA_RANK]
    gd = zm[:, 3 * RW_W + DECAY_RANK + AAA_RANK:RW_COLS]
    lane = lax.broadcasted_iota(jnp.int32, xwa.shape, 1)
    xwa = jnp.where(lane < DECAY_RANK, jnp.tanh(xwa), xwa)
    lora = jnp.dot(xwa.astype(BF16), w2a2_ref[...], preferred_element_type=F32)
    yield VPU_STAGE
    lw = -math.exp(-0.5) * jax.nn.sigmoid(w0_ref[...] + lora[:, 0:RW_W])
    a = jax.nn.sigmoid(a0_ref[...] + lora[:, RW_W:2 * RW_W])
    gate = jnp.dot(jax.nn.sigmoid(gd).astype(BF16), g2_ref[...], preferred_element_type=F32)
    yield VPU_STAGE

    def segsum(x):
        return sum(jnp.dot(part, seg_ref[...], preferred_element_type=F32) for part in _split(x, SEG_PASSES))

    kkn = k * kk_ref[...]
    kkn = kkn * lax.rsqrt(jnp.maximum(segsum(kkn * kkn), 1e-24))
    yield VPU_STAGE
    k2 = k * (1.0 + (a - 1.0) * ka_ref[...])
    if t_valid < tt:
        valid = row < t_valid
        lw = jnp.where(valid, lw, 0.0)
        kkn = jnp.where(valid, kkn, 0.0)
        k2 = jnp.where(valid, k2, 0.0)
        v = jnp.where(valid, v, 0.0)

    c = RW_CHUNK
    ti = lax.broadcasted_iota(jnp.int32, (tt, tt), 0)
    tj = lax.broadcasted_iota(jnp.int32, (tt, tt), 1)
    cum = jnp.where(((ti >> 6) == (tj >> 6)) & (ti >= tj), 1.0, 0.0).astype(BF16)
    g = sum(jnp.dot(cum, part, preferred_element_type=F32) for part in _split(lw, CUM_PIECES))
    gc = jnp.concatenate([jnp.broadcast_to(g[e - 1:e, :], (c, RW_W)) for e in range(c, tt + 1, c)], axis=0)
    yield VPU_STAGE
    ieg = jnp.exp(-g)
    egc = jnp.exp(gc - g)
    b = kkn * a
    at = -kkn * jnp.exp(g - lw)
    rt = r * jnp.exp(g)
    yield VPU_STAGE
    bh, kh = b * ieg, k2 * ieg
    bb, kb = b * egc, k2 * egc
    gam = jnp.exp(gc)
    yield VPU_STAGE

    probs = [(ci, p) for ci in range(tt // c) for p in range(N_PAIRS)]

    def blk(x, ci, p):
        return x[ci * c:(ci + 1) * c, p * RW_PAIR:(p + 1) * RW_PAIR]

    head0 = lax.broadcasted_iota(jnp.int32, (c, RW_PAIR), 1) < RW_HD

    def stack(x):
        return jnp.concatenate([jnp.where(head0, x, 0.0), jnp.where(head0, 0.0, x)], axis=0)

    def twice(x):
        return jnp.concatenate([x, x], axis=0)

    n = 2 * c
    ii = lax.broadcasted_iota(jnp.int32, (n, n), 0)
    jj = lax.broadcasted_iota(jnp.int32, (n, n), 1)
    same_head = (ii >> 6) == (jj >> 6)
    strict = same_head & (ii > jj)
    incl = same_head & (ii >= jj)

    ats = [stack(blk(at, *q)) for q in probs]
    rts = [stack(blk(rt, *q)) for q in probs]
    vs = [stack(blk(v, *q)) for q in probs]
    aa = [_mm(jnp.concatenate([a_, r_], axis=0),
              jnp.concatenate([twice(blk(bh, *q)), twice(blk(kh, *q))], axis=0), NT, CHUNK_PASSES)
          for a_, r_, q in zip(ats, rts, probs)]
    yield MXU_STAGE
    dab = [jnp.where(strict, x[0:n, 0:n], 0.0) for x in aa]
    dak = [jnp.where(strict, x[0:n, n:2 * n], 0.0) for x in aa]
    drbk = [jnp.concatenate([jnp.where(incl, x[n:2 * n, 0:n], 0.0),
                             jnp.where(incl, x[n:2 * n, n:2 * n], 0.0)], axis=1) for x in aa]
    dv = [_mm(d, v_, NN, CHUNK_PASSES) for d, v_ in zip(dak, vs)]
    tinv = yield from _neumann_inverse(dab, INV_PASSES)
    gh = [_mm(t_, jnp.concatenate([a_, d], axis=1), NN, INV_PASSES) for t_, a_, d in zip(tinv, ats, dv)]
    yield MXU_STAGE
    zeros = jnp.zeros((n, n), F32)
    qy = [_mm(d, jnp.concatenate([x, jnp.concatenate([zeros, v_], axis=1)], axis=0), NN, CHUNK_PASSES)
          for d, x, v_ in zip(drbk, gh, vs)]
    yield MXU_STAGE
    qe = [r_ + x[:, 0:n] for r_, x in zip(rts, qy)]
    y0 = [x[:, n:2 * n] for x in qy]
    bbs = [stack(blk(bb, *q)) for q in probs]
    kbs = [stack(blk(kb, *q)) for q in probs]
    mz = [_mm(jnp.concatenate([x, jnp.concatenate([zeros, v_], axis=1)], axis=0),
              jnp.concatenate([b_, k_], axis=0), TN, STATE_PASSES)
          for x, v_, b_, k_ in zip(gh, vs, bbs, kbs)]
    yield MXU_STAGE
    eye = ii == jj
    mx = [jnp.where(eye, blk(gam, *q)[0:1, :], 0.0) + x[0:n] for x, q in zip(mz, probs)]
    zz = [x[n:2 * n] for x in mz]

    assert order[0] == r0, "sub-tiles must reach their state update in time order"
    state = [s_s[p] for p in range(N_PAIRS)]
    y_rows = []
    for ci in range(tt // c):
        y_pairs = []
        for p in range(N_PAIRS):
            i = ci * N_PAIRS + p
            ys = _mm(qe[i], state[p], NT, STATE_PASSES) + y0[i]
            y_pairs.append(ys[0:c] + ys[c:n])
            state[p] = _mm(state[p], mx[i], NN, STATE_PASSES) + zz[i]
        y_rows.append(jnp.concatenate(y_pairs, axis=1))
    for p in range(N_PAIRS):
        s_s[p] = state[p]
    order[0] = r0 + tt
    yield MXU_STAGE
    y = jnp.concatenate(y_rows, axis=0)

    inv_hd = 1.0 / RW_HD
    d = y - segsum(y) * inv_hd
    yield VPU_STAGE
    yn = d * lax.rsqrt(segsum(d * d) * inv_hd + LNX_EPS) * lnw_ref[...] + lnb_ref[...]
    yn = yn + segsum(r * k2 * rk_ref[...]) * v
    y_ref[0, r0:r0 + tt, :] = (yn * gate).astype(y_ref.dtype)


def _rwkv_kernel(zr_ref, shift_ref, s0_ref, mu_ref, w0_ref, a0_ref, w2a2_ref, g2_ref, kk_ref, ka_ref,
                 rk_ref, lnw_ref, lnb_ref, seg_ref, y_ref, sfin_ref, s_s, prev_s, *, tt, sub, t_valid):
    t = pl.program_id(1)

    @pl.when(t == 0)
    def _():
        s_s[...] = s0_ref[0]
        prev_s[0:1, :] = shift_ref[0]

    order = [0]
    pending = [_rwkv_rows(zr_ref, mu_ref, w0_ref, a0_ref, w2a2_ref, g2_ref, kk_ref, ka_ref, rk_ref, lnw_ref,
                          lnb_ref, seg_ref, y_ref, s_s, prev_s, r0, sub, t_valid, order)
               for r0 in range(0, tt, sub)]
    active = []
    while pending or active:
        if pending and all(stage == MXU_STAGE for _, stage in active):
            active.append([pending.pop(0), VPU_STAGE])
        for item in list(active):
            try:
                item[1] = next(item[0])
            except StopIteration:
                active.remove(item)
    prev_s[0:1, :] = zr_ref[0, tt - 1:tt, :]

    @pl.when(t == pl.num_programs(1) - 1)
    def _():
        sfin_ref[0] = s_s[...]


def _rwkv(zr, shift0, s0_pairs, lp, t_valid):
    b, tp, _ = zr.shape
    tt = _row_tile(tp, RW_TILE)
    assert t_valid == tp or tp == tt, "padding is only supported inside a single time tile"
    vec = lambda w: _const_spec((1, w))
    return pl.pallas_call(
        functools.partial(_rwkv_kernel, tt=tt, sub=min(tt, RW_SUB), t_valid=tt if t_valid == tp else t_valid),
        grid=(b, tp // tt),
        in_specs=[
            pl.BlockSpec((1, tt, RW_COLS), lambda i, t: (i, t, 0)),
            pl.BlockSpec((1, 1, RW_COLS), lambda i, t: (i, 0, 0)),
            pl.BlockSpec((1, N_PAIRS, RW_PAIR, RW_PAIR), lambda i, t: (i, 0, 0, 0)),
            vec(RW_COLS), vec(RW_W), vec(RW_W),
            _const_spec((DECAY_RANK + AAA_RANK, 2 * RW_W)), _const_spec((GATE_RANK, RW_W)),
            vec(RW_W), vec(RW_W), vec(RW_W), vec(RW_W), vec(RW_W),
            _const_spec((RW_W, RW_W)),
        ],
        out_specs=[
            pl.BlockSpec((1, tt, RW_W), lambda i, t: (i, t, 0)),
            pl.BlockSpec((1, N_PAIRS, RW_PAIR, RW_PAIR), lambda i, t: (i, 0, 0, 0)),
        ],
        out_shape=[
            jax.ShapeDtypeStruct((b, tp, RW_W), BF16),
            jax.ShapeDtypeStruct((b, N_PAIRS, RW_PAIR, RW_PAIR), F32),
        ],
        scratch_shapes=[
            pltpu.VMEM((N_PAIRS, RW_PAIR, RW_PAIR), F32),
            pltpu.VMEM((8, RW_COLS), F32),
        ],
        compiler_params=pltpu.CompilerParams(
            dimension_semantics=("parallel", "arbitrary"), vmem_limit_bytes=48 << 20),
        name="rwkv",
    )(zr, shift0, s0_pairs, lp["mu"], lp["w0"], lp["a0"], lp["w2a2"], lp["g2"], lp["k_k"], lp["k_a"],
      lp["r_k"], lp["lnx_w"], lp["lnx_b"], lp["seg"])


def _to_pairs(s):
    b = s.shape[0]
    s = s.reshape(b, N_PAIRS, 2, RW_HD, RW_HD)
    eye = jnp.eye(2, dtype=s.dtype)
    return jnp.einsum("bphvk,hg->bphvgk", s, eye).reshape(b, N_PAIRS, RW_PAIR, RW_PAIR)


def _from_pairs(sp):
    b = sp.shape[0]
    sp = sp.reshape(b, N_PAIRS, 2, RW_HD, 2, RW_HD)
    return jnp.stack([sp[:, :, 0, :, 0, :], sp[:, :, 1, :, 1, :]], axis=2).reshape(b, RW_HEADS, RW_HD, RW_HD)


def _stack_maps(q):
    lane = lax.broadcasted_iota(jnp.int32, q.shape, 1)
    zero = jnp.zeros_like(q)
    return jnp.concatenate([jnp.where(lane < DF_HD, q, zero), jnp.where(lane < DF_HD, zero, q)], axis=0)


def _attn_prompt_kernel(sc_ref, q_ref, k_ref, kpos_ref, vt_ref, g_ref, dbias_ref, o_ref, m_s, acc_s, s_s, smax_s,
                        *, tb):
    h = pl.program_id(1)
    i = pl.program_id(2)
    lam, out_scale, slope = sc_ref[0], sc_ref[1], sc_ref[2 + h]
    qt = q_ref[0].astype(F32).T
    dim = lax.broadcasted_iota(jnp.int32, (DF_VD, tb), 0)
    q_pos = jnp.where(dim == 0, slope * POS_SPLIT, jnp.where(dim == 1, slope, 0.0))
    qs_t = jnp.concatenate(
        [jnp.concatenate([jnp.where(dim < DF_HD, qt, 0.0), jnp.where(dim < DF_HD, 0.0, qt)], axis=1),
         jnp.concatenate([q_pos, q_pos], axis=1)], axis=0).astype(BF16)

    def scores(j):
        keys = jnp.concatenate([k_ref[0, j], kpos_ref[j]], axis=1)
        return jnp.dot(keys, qs_t, preferred_element_type=F32)

    ones = jnp.ones((ONES_ROWS, tb), BF16)

    def weighted_values(j, p):
        return jnp.dot(jnp.concatenate([vt_ref[0, j], ones], axis=0), p, preferred_element_type=F32)

    def stash_scores(j):
        sc = scores(j)
        s_s[...] = sc
        smax_s[...] = jnp.max(sc, axis=0, keepdims=True)

    stash_scores(0)
    s = scores(i) + dbias_ref[0]
    m = jnp.max(s, axis=0, keepdims=True)
    m_s[...] = m
    acc_s[...] = weighted_values(i, jnp.exp(s - m).astype(BF16))

    def step(j, stash_next):
        s = s_s[...]
        m_old = m_s[...]
        m_new = jnp.maximum(m_old, smax_s[...])
        if stash_next:
            stash_scores(j + 1)
        p = jnp.exp(s - m_new).astype(BF16)
        acc_s[...] = jnp.exp(m_old - m_new) * acc_s[...] + weighted_values(j, p)
        m_s[...] = m_new

    n_lead = jnp.maximum(i - 1, 0)
    n_trips = n_lead // ATTN_TRIP

    def trip(jj, carry):
        for u in range(ATTN_TRIP):
            step(ATTN_TRIP * jj + u, True)
        return carry

    lax.fori_loop(0, n_trips, trip, 0)
    for u in range(ATTN_TRIP - 1):
        @pl.when(n_lead - ATTN_TRIP * n_trips > u)
        def _():
            step(ATTN_TRIP * n_trips + u, True)

    @pl.when(i > 0)
    def _():
        step(i - 1, False)

    acc = acc_s[...]
    on = acc[0:DF_VD] / acc[DF_VD:DF_VD + 1]
    ot = on[:, 0:tb] - lam * on[:, tb:2 * tb]
    ot = ot * lax.rsqrt(jnp.mean(ot * ot, axis=0, keepdims=True) + SUBLN_EPS)
    o_ref[0] = (ot.T * g_ref[...] * out_scale).astype(o_ref.dtype)


def _attn_prompt(scal, q, k, vt, g, tb):
    b, t, _ = q.shape
    nb = t // tb
    pos = jnp.arange(t, dtype=jnp.int32).reshape(nb, tb, 1)
    lane = jnp.arange(V7X_LANES, dtype=jnp.int32)
    kpos = jnp.where(lane == 0, pos // POS_SPLIT, jnp.where(lane == 1, pos % POS_SPLIT, 0)).astype(BF16)
    kl = jnp.arange(tb, dtype=jnp.int32)[:, None]
    ql = jnp.arange(2 * tb, dtype=jnp.int32)[None, :] % tb
    after = jnp.maximum(kl - ql, 0).astype(F32)
    dbias = jnp.where((kl // CHUNK) <= (ql // CHUNK), -2.0 * scal[2:2 + DF_HEADS, None, None] * after, NEG_BIG)
    blk = lambda: pl.BlockSpec((1, tb, DF_VD), lambda bi, h, i: (bi, i, h))
    return pl.pallas_call(
        functools.partial(_attn_prompt_kernel, tb=tb),
        grid=(b, DF_HEADS, nb),
        in_specs=[pl.BlockSpec(memory_space=pltpu.SMEM), blk(),
                  pl.BlockSpec((1, nb, tb, DF_VD), lambda bi, h, i: (bi, 0, 0, h)),
                  _const_spec((nb, tb, V7X_LANES)),
                  pl.BlockSpec((1, nb, DF_VD, tb), lambda bi, h, i: (bi, 0, h, 0)),
                  _const_spec((1, DF_VD)),
                  pl.BlockSpec((1, tb, 2 * tb), lambda bi, h, i: (h, 0, 0))],
        out_specs=blk(),
        out_shape=jax.ShapeDtypeStruct((b, t, DF_V), BF16),
        scratch_shapes=[pltpu.VMEM((1, 2 * tb), F32),
                        pltpu.VMEM((DF_VD + ONES_ROWS, 2 * tb), F32),
                        pltpu.VMEM((tb, 2 * tb), F32),
                        pltpu.VMEM((1, 2 * tb), F32)],
        compiler_params=pltpu.CompilerParams(
            dimension_semantics=("parallel", "parallel", "arbitrary"), vmem_limit_bytes=32 << 20),
        name="attn_prompt",
    )(scal, q, k, kpos, vt, g, dbias)


def _attn_sample_kernel(sc_ref, q_ref, kn_ref, vn_ref, kp_ref, vp_ref, g_ref, o_ref, *, t, past):
    lam, out_scale = sc_ref[0], sc_ref[1]
    for h in range(DF_HEADS):
        slope = sc_ref[2 + h]
        lanes = slice(h * DF_VD, (h + 1) * DF_VD)
        qs = _stack_maps(q_ref[0, :, lanes])

        def scores(keys, k0):
            nk = keys.shape[0]
            s = lax.dot_general(qs, keys.astype(BF16), NT, preferred_element_type=F32)
            qpos = past + lax.rem(lax.broadcasted_iota(jnp.int32, (2 * t, nk), 0), t)
            kpos = k0 + lax.broadcasted_iota(jnp.int32, (2 * t, nk), 1)
            s = s - slope * jnp.abs(qpos - kpos).astype(F32)
            return jnp.where((kpos >> 6) <= (qpos >> 6), s, NEG_BIG)

        sp = scores(kp_ref[0, :, h, :], 0)
        sn = scores(kn_ref[0, :, lanes], past)
        m = jnp.maximum(jnp.max(sp, axis=-1, keepdims=True), jnp.max(sn, axis=-1, keepdims=True))
        pp = jnp.exp(sp - m)
        pn = jnp.exp(sn - m)
        l = jnp.sum(pp, axis=-1, keepdims=True) + jnp.sum(pn, axis=-1, keepdims=True)
        acc = (jnp.dot(pp.astype(BF16), vp_ref[0, :, h, :].astype(BF16), preferred_element_type=F32)
               + jnp.dot(pn.astype(BF16), vn_ref[0, :, lanes].astype(BF16), preferred_element_type=F32))
        on = acc / l
        o = on[0:t] - lam * on[t:2 * t]
        o_ref[0, :, lanes] = (_rms(o, g_ref[...], SUBLN_EPS) * out_scale).astype(o_ref.dtype)


def _attn_sample(scal, q, k, v, cache_k, cache_v, layer, g):
    b, t, _ = q.shape
    past = cache_k.shape[2]
    new = lambda: pl.BlockSpec((1, t, DF_V), lambda bi: (bi, 0, 0))
    old = lambda: pl.BlockSpec((None, 1, past, DF_HEADS, DF_VD), lambda bi: (layer, bi, 0, 0, 0))
    return pl.pallas_call(
        functools.partial(_attn_sample_kernel, t=t, past=past),
        grid=(b,),
        in_specs=[pl.BlockSpec(memory_space=pltpu.SMEM), new(), new(), new(), old(), old(),
                  _const_spec((1, DF_VD))],
        out_specs=new(),
        out_shape=jax.ShapeDtypeStruct((b, t, DF_V), BF16),
        compiler_params=pltpu.CompilerParams(dimension_semantics=("parallel",), vmem_limit_bytes=48 << 20),
        name="attn_sample",
    )(scal, q, k, v, cache_k, cache_v, g)


FF_STEP = 256


def _mlp_kernel(x_ref, y_ref, o_ref, wo_ref, g_ref, wg_ref, wu_ref, wd_ref, gf_ref, out_ref, h_s, *, final_norm):
    x1 = (x_ref[...]
          + jnp.dot(y_ref[...], wo_ref[0:RW_W, :], preferred_element_type=F32)
          + jnp.dot(o_ref[...], wo_ref[RW_W:RW_W + DF_V, :], preferred_element_type=F32))
    out_ref[...] = x1
    h_s[...] = _rms(x1, g_ref[...], RMS_EPS).astype(BF16)
    for c in range(0, D_FF, FF_STEP):
        h2 = h_s[...]
        gt = jnp.dot(h2, wg_ref[:, c:c + FF_STEP], preferred_element_type=F32)
        up = jnp.dot(h2, wu_ref[:, c:c + FF_STEP], preferred_element_type=F32)
        act = (gt * jax.nn.sigmoid(gt) * up).astype(BF16)
        out_ref[...] += jnp.dot(act, wd_ref[c:c + FF_STEP, :], preferred_element_type=F32)
    if final_norm:
        out_ref[...] = _rms(out_ref[...], gf_ref[...], RMS_EPS)


def _mlp(x2d, y_rw, o_df, wo, g, wg, wu, wd, final_g, final_norm):
    n = x2d.shape[0]
    tm = _row_tile(n, 512)
    row = lambda w: pl.BlockSpec((tm, w), lambda i: (i, 0))
    return pl.pallas_call(
        functools.partial(_mlp_kernel, final_norm=final_norm),
        grid=(n // tm,),
        in_specs=[row(D_MODEL), row(RW_W), row(DF_V), _const_spec((RW_W + DF_V, D_MODEL)),
                  _const_spec((1, D_MODEL)), _const_spec((D_MODEL, D_FF)), _const_spec((D_MODEL, D_FF)),
                  _const_spec((D_FF, D_MODEL)), _const_spec((1, D_MODEL))],
        out_specs=row(D_MODEL),
        out_shape=jax.ShapeDtypeStruct((n, D_MODEL), F32),
        scratch_shapes=[pltpu.VMEM((tm, D_MODEL), BF16)],
        compiler_params=_vmem_limit(52 << 20),
        name="mlp",
    )(x2d, y_rw, o_df, wo, g.reshape(1, D_MODEL), wg, wu, wd, final_g.reshape(1, D_MODEL))


def _layer(x, shift0, s0_pairs, attend, lp, vt_block, layer, depth, kv_all):
    b, t, _ = x.shape
    n = b * t
    zr, q, k_all, v_all, kb, vb = _inproj(x.reshape(n, D_MODEL), lp["norm1"], lp["w_in"], vt_block, layer, depth,
                                          kv_all)
    zr = zr.reshape(b, t, RW_COLS)
    tp = -(-t // RW_CHUNK) * RW_CHUNK
    zr_p = zr if tp == t else jnp.pad(zr, ((0, 0), (0, tp - t), (0, 0)))
    y_rw, s_fin = _rwkv(zr_p, shift0, s0_pairs, lp, t)
    if tp != t:
        y_rw = y_rw[:, :t]
    o = attend(q.reshape(b, t, DF_QK), kb, vb)
    x_new = _mlp(x.reshape(n, D_MODEL), y_rw.reshape(n, RW_W), o.reshape(n, DF_V), lp["w_out"], lp["norm2"],
                 lp["wg"], lp["wu"], lp["wd"], lp["final_g"], layer == depth - 1)
    return x_new.reshape(b, t, D_MODEL), zr[:, t - 1:t], s_fin, (k_all, v_all)


@jax.jit
def kernel(x_prompt, x_sample, cache_k, cache_v, state_wkv, state_shift, norm1_g, w_in, rw_mu, rw_w0, rw_w2,
           rw_a0, rw_a2, rw_g2, rw_k_k, rw_k_a, rw_r_k, rw_lnx_w, rw_lnx_b, df_lq1, df_lk1, df_lq2, df_lk2,
           df_subln_g, w_out, norm2_g, ffn_w_gate, ffn_w_up, ffn_w_down, final_g):
    depth = w_in.shape[0]
    bp, tp = x_prompt.shape[0], x_prompt.shape[1]
    bs, ts = x_sample.shape[0], x_sample.shape[1]
    past = cache_k.shape[2]
    tb = _row_tile(tp, ATTN_BLOCK)
    assert tb & (tb - 1) == 0 and tb % CHUNK == 0
    slopes = 2.0 ** (-8.0 * jnp.arange(1, DF_HEADS + 1, dtype=F32) / DF_HEADS)
    head = jnp.arange(RW_W) // RW_HD
    seg = (head[:, None] == head[None, :]).astype(BF16)
    zero_blk = jnp.zeros((DECAY_RANK, RW_W), F32)
    w_in_b, w_out_b = w_in.astype(BF16), w_out.astype(BF16)
    wg_b, wu_b, wd_b = ffn_w_gate.astype(BF16), ffn_w_up.astype(BF16), ffn_w_down.astype(BF16)

    xp, xs = x_prompt, x_sample
    kv_buffers = lambda n: tuple(jnp.zeros((depth, n, DF_HEADS, DF_VD), F32) for _ in range(2))
    kv_p, kv_s = kv_buffers(bp * tp), kv_buffers(bs * ts)
    outs = [[] for _ in range(4)]
    for l in range(depth):
        lam_init = 0.8 - 0.6 * math.exp(-0.3 * l)
        lam = (jnp.exp(jnp.sum((df_lq1[l] * df_lk1[l]).astype(F32)))
               - jnp.exp(jnp.sum((df_lq2[l] * df_lk2[l]).astype(F32))) + lam_init)
        scal = jnp.concatenate([jnp.stack([lam, jnp.asarray(1.0 - lam_init, F32)]), slopes]).astype(F32)
        row = lambda u: u[l].reshape(1, -1)
        lp = dict(
            norm1=norm1_g[l], w_in=w_in_b[l], mu=row(rw_mu), w0=row(rw_w0), a0=row(rw_a0),
            w2a2=jnp.concatenate([jnp.concatenate([rw_w2[l], zero_blk], axis=1),
                                  jnp.concatenate([zero_blk, rw_a2[l]], axis=1)], axis=0).astype(BF16),
            g2=rw_g2[l].astype(BF16), k_k=row(rw_k_k), k_a=row(rw_k_a), r_k=row(rw_r_k),
            lnx_w=row(rw_lnx_w), lnx_b=row(rw_lnx_b), seg=seg, w_out=w_out_b[l], norm2=norm2_g[l],
            wg=wg_b[l], wu=wu_b[l], wd=wd_b[l], final_g=final_g)
        g_sub = df_subln_g[l].reshape(1, DF_VD)

        att_p = lambda q, kb, vt: _attn_prompt(scal, q, kb.reshape(bp, tp // tb, tb, DF_QK),
                                               vt.reshape(bp, tp // tb, DF_V, tb), g_sub, tb)
        xp, shp, sp, kv_p = _layer(xp, jnp.zeros((bp, 1, RW_COLS), F32),
                                   jnp.zeros((bp, N_PAIRS, RW_PAIR, RW_PAIR), F32), att_p, lp, tb, l, depth, kv_p)
        att_s = lambda q, kb, vb: _attn_sample(scal, q, kb.reshape(bs, ts, DF_QK), vb.reshape(bs, ts, DF_V),
                                               cache_k, cache_v, l, g_sub)
        xs, shs, ss, kv_s = _layer(xs, state_shift[l], _to_pairs(state_wkv[l]), att_s, lp, None, l, depth, kv_s)
        for lst, val in zip(outs, (_from_pairs(sp), shp, _from_pairs(ss), shs)):
            lst.append(val)

    wkv_p, shift_p, wkv_s, shift_s = (jnp.stack(lst) for lst in outs)
    per_token = lambda u, b, t: u.reshape(depth, b, t, DF_HEADS, DF_VD)
    return (xp, xs, per_token(kv_p[0], bp, tp), per_token(kv_p[1], bp, tp), wkv_p, shift_p,
            per_token(kv_s[0], bs, ts), per_token(kv_s[1], bs, ts), wkv_s, shift_s)
```

```python
import functools
import math

import jax
import jax.numpy as jnp
from jax import lax
from jax.experimental import pallas as pl
from jax.experimental.pallas import tpu as pltpu

F32 = jnp.float32
BF16 = jnp.bfloat16

D_MODEL = 1024
CHUNK = 64
RW_HEADS = 8
RW_HD = 64
RW_W = RW_HEADS * RW_HD
DECAY_RANK = 64
AAA_RANK = 64
GATE_RANK = 128
RW_COLS = 3 * RW_W + DECAY_RANK + AAA_RANK + GATE_RANK
DF_HEADS = 4
DF_HD = 64
DF_VD = 2 * DF_HD
DF_QK = DF_HEADS * 2 * DF_HD
DF_V = DF_HEADS * DF_VD
P_IN = RW_COLS + 2 * DF_QK + DF_V
D_FF = 2816
RMS_EPS = 1e-6
LNX_EPS = 64e-5
SUBLN_EPS = 1e-5
QK_SCALE = DF_HD ** -0.5

V7X_LANES = 128
V7X_MXU_COLS = 256
V7X_VMEM_BYTES = 64 << 20
VMEM_LIMIT_ROWS = (V7X_VMEM_BYTES * 3) // 4
VMEM_LIMIT_MLP = (V7X_VMEM_BYTES * 13) // 16
VMEM_LIMIT_ATTN = V7X_VMEM_BYTES // 2
ATTN_BLOCK = 512
POS_SPLIT = 256
ONES_ROWS = 16
ATTN_TRIP = 4
RW_CHUNK = 64
RW_TILE = 1024
RW_SUB = 128
RW_PAIR = 2 * RW_HD
N_PAIRS = RW_HEADS // 2
NEG_BIG = -1e30

INV_PASSES = 1
CHUNK_PASSES = 1
STATE_PASSES = 1
SEG_PASSES = 1
CUM_PIECES = 2

VPU_STAGE, MXU_STAGE = "vpu", "mxu"

NN = (((1,), (0,)), ((), ()))
NT = (((1,), (1,)), ((), ()))
TN = (((0,), (0,)), ((), ()))


def _vmem_limit(nbytes):
    return pltpu.CompilerParams(vmem_limit_bytes=int(nbytes))


def _row_tile(n, target):
    t = min(n, target)
    while n % t:
        t -= 8
    return t


def _const_spec(shape):
    zeros = (0,) * len(shape)
    return pl.BlockSpec(shape, lambda *_: zeros, pipeline_mode=pl.Buffered(1))


def _split(x, parts):
    out, rem = [], x
    for i in range(parts):
        hi = rem.astype(BF16)
        out.append(hi)
        if i + 1 < parts:
            rem = rem - hi.astype(F32)
    return out


def _mm(a, b, dims=NN, passes=1):
    dot = functools.partial(lax.dot_general, dimension_numbers=dims, preferred_element_type=F32)
    if passes == 1:
        return dot(a.astype(BF16), b.astype(BF16))
    ah, al = _split(a, 2)
    if passes == 2:
        bh = b.astype(BF16)
        return dot(ah, bh) + dot(al, bh)
    bh, bl = _split(b, 2)
    return dot(ah, bh) + dot(al, bh) + dot(ah, bl)


def _rms(x, g, eps):
    return x * lax.rsqrt(jnp.mean(x * x, axis=-1, keepdims=True) + eps) * g


def _inproj_kernel(x_ref, g_ref, w_ref, k_all_ref, v_all_ref, zr_ref, q_ref, k_ref, v_ref, kb_ref, vb_ref, *, vt_block):
    del k_all_ref, v_all_ref
    h = _rms(x_ref[...], g_ref[...], RMS_EPS).astype(BF16)
    step = V7X_MXU_COLS

    def proj(c0):
        return jnp.dot(h, w_ref[:, c0:c0 + step], preferred_element_type=F32)

    for c in range(0, RW_COLS, step):
        zr_ref[:, c:c + step] = proj(c)
    for c in range(0, DF_QK, step):
        q_ref[:, c:c + step] = (proj(RW_COLS + c) * QK_SCALE).astype(q_ref.dtype)
        kc = proj(RW_COLS + DF_QK + c)
        for hh in range(step // DF_VD):
            k_ref[:, c // DF_VD + hh, :] = kc[:, hh * DF_VD:(hh + 1) * DF_VD]
        kb_ref[:, c:c + step] = kc.astype(kb_ref.dtype)
        vc = proj(RW_COLS + 2 * DF_QK + c)
        for hh in range(step // DF_VD):
            v_ref[:, c // DF_VD + hh, :] = vc[:, hh * DF_VD:(hh + 1) * DF_VD]
        if vt_block is None:
            vb_ref[:, c:c + step] = vc.astype(vb_ref.dtype)
        else:
            for rb in range(vb_ref.shape[0]):
                vb_ref[rb, c:c + step, :] = vc[rb * vt_block:(rb + 1) * vt_block, :].T.astype(vb_ref.dtype)


def _inproj(x2d, g, w_bf, vt_block, layer, depth, kv_all):
    n = x2d.shape[0]
    tm = _row_tile(n, 512)
    row = lambda w: pl.BlockSpec((tm, w), lambda i: (i, 0))
    heads = lambda: pl.BlockSpec((None, tm, DF_HEADS, DF_VD), lambda i: (layer, i, 0, 0))
    if vt_block is None:
        vb_spec, vb_shape = row(DF_V), (n, DF_V)
    else:
        assert tm % vt_block == 0
        vb_spec = pl.BlockSpec((tm // vt_block, DF_V, vt_block), lambda i: (i, 0, 0))
        vb_shape = (n // vt_block, DF_V, vt_block)
    return pl.pallas_call(
        functools.partial(_inproj_kernel, vt_block=vt_block),
        grid=(n // tm,),
        in_specs=[row(D_MODEL), _const_spec((1, D_MODEL)), _const_spec((D_MODEL, P_IN))]
        + [pl.BlockSpec(memory_space=pl.ANY), pl.BlockSpec(memory_space=pl.ANY)],
        out_specs=[row(RW_COLS), row(DF_QK), heads(), heads(), row(DF_QK), vb_spec],
        out_shape=[
            jax.ShapeDtypeStruct((n, RW_COLS), F32),
            jax.ShapeDtypeStruct((n, DF_QK), BF16),
            jax.ShapeDtypeStruct((depth, n, DF_HEADS, 2 * DF_HD), F32),
            jax.ShapeDtypeStruct((depth, n, DF_HEADS, DF_VD), F32),
            jax.ShapeDtypeStruct((n, DF_QK), BF16),
            jax.ShapeDtypeStruct(vb_shape, BF16),
        ],
        input_output_aliases={3: 2, 4: 3},
        compiler_params=_vmem_limit(VMEM_LIMIT_ROWS),
        name="inproj",
    )(x2d, g.reshape(1, D_MODEL), w_bf, *kv_all)


def _neumann_inverse(dab, passes):
    n = dab[0].shape[0]
    ii = lax.broadcasted_iota(jnp.int32, (n, n), 0)
    jj = lax.broadcasted_iota(jnp.int32, (n, n), 1)
    same16 = (ii >> 4) == (jj >> 4)
    same32 = (ii >> 5) == (jj >> 5)
    eye = jnp.where(ii == jj, 1.0, 0.0)
    mm = functools.partial(_mm, passes=passes)
    x = [jnp.where(same16, d, 0.0) for d in dab]
    p = [eye + xi for xi in x]
    x = [mm(xi, xi) for xi in x]
    yield MXU_STAGE
    for _ in range(2):
        xp = [mm(xi, jnp.concatenate([xi, pi], axis=1)) for xi, pi in zip(x, p)]
        p = [pi + r[:, n:] for pi, r in zip(p, xp)]
        x = [r[:, :n] for r in xp]
        yield MXU_STAGE
    p = [pi + mm(xi, pi) for xi, pi in zip(x, p)]
    yield MXU_STAGE
    for off_mask in (same32 & jnp.logical_not(same16), jnp.logical_not(same32)):
        off = [jnp.where(off_mask, d, 0.0) for d in dab]
        po = [mm(pi, oi) for pi, oi in zip(p, off)]
        yield MXU_STAGE
        p = [pi + mm(qi, pi) for pi, qi in zip(p, po)]
        yield MXU_STAGE
    return p


def _rwkv_rows(zr_ref, mu_ref, w0_ref, a0_ref, w2a2_ref, g2_ref, kk_ref, ka_ref, rk_ref, lnw_ref, lnb_ref, seg_ref,
               y_ref, s_s, prev_s, r0, tt, t_valid, order):
    zr = zr_ref[0, r0:r0 + tt, :]
    row = lax.broadcasted_iota(jnp.int32, (tt, 1), 0)
    before = prev_s[0:1, :] if r0 == 0 else zr_ref[0, r0 - 1:r0, :]
    zprev = jnp.where(row == 0, before, pltpu.roll(zr, 1, axis=0))
    zm = zr + (zprev - zr) * mu_ref[...]
    yield VPU_STAGE
    r = zm[:, 0:RW_W]
    k = zm[:, RW_W:2 * RW_W]
    v = zm[:, 2 * RW_W:3 * RW_W]
    xwa = zm[:, 3 * RW_W:3 * RW_W + DECAY_RANK + AAA_RANK]
    gd = zm[:, 3 * RW_W + DECAY_RANK + AAA_RANK:RW_COLS]
    lane = lax.broadcasted_iota(jnp.int32, xwa.shape, 1)
    xwa = jnp.where(lane < DECAY_RANK, jnp.tanh(xwa), xwa)
    lora = jnp.dot(xwa.astype(BF16), w2a2_ref[...], preferred_element_type=F32)
    yield VPU_STAGE
    lw = -math.exp(-0.5) * jax.nn.sigmoid(w0_ref[...] + lora[:, 0:RW_W])
    a = jax.nn.sigmoid(a0_ref[...] + lora[:, RW_W:2 * RW_W])
    gate = jnp.dot(jax.nn.sigmoid(gd).astype(BF16), g2_ref[...], preferred_element_type=F32)
    yield VPU_STAGE

    def segsum(x):
        return sum(jnp.dot(part, seg_ref[...], preferred_element_type=F32) for part in _split(x, SEG_PASSES))

    kkn = k * kk_ref[...]
    kkn = kkn * lax.rsqrt(jnp.maximum(segsum(kkn * kkn), 1e-24))
    yield VPU_STAGE
    k2 = k * (1.0 + (a - 1.0) * ka_ref[...])
    if t_valid < tt:
        valid = row < t_valid
        lw = jnp.where(valid, lw, 0.0)
        kkn = jnp.where(valid, kkn, 0.0)
        k2 = jnp.where(valid, k2, 0.0)
        v = jnp.where(valid, v, 0.0)

    c = RW_CHUNK
    ti = lax.broadcasted_iota(jnp.int32, (tt, tt), 0)
    tj = lax.broadcasted_iota(jnp.int32, (tt, tt), 1)
    cum = jnp.where(((ti >> 6) == (tj >> 6)) & (ti >= tj), 1.0, 0.0).astype(BF16)
    g = sum(jnp.dot(cum, part, preferred_element_type=F32) for part in _split(lw, CUM_PIECES))
    gc = jnp.concatenate([jnp.broadcast_to(g[e - 1:e, :], (c, RW_W)) for e in range(c, tt + 1, c)], axis=0)
    yield VPU_STAGE
    ieg = jnp.exp(-g)
    egc = jnp.exp(gc - g)
    b = kkn * a
    at = -kkn * jnp.exp(g - lw)
    rt = r * jnp.exp(g)
    yield VPU_STAGE
    bh, kh = b * ieg, k2 * ieg
    bb, kb = b * egc, k2 * egc
    gam = jnp.exp(gc)
    yield VPU_STAGE

    probs = [(ci, p) for ci in range(tt // c) for p in range(N_PAIRS)]

    def blk(x, ci, p):
        return x[ci * c:(ci + 1) * c, p * RW_PAIR:(p + 1) * RW_PAIR]

    head0 = lax.broadcasted_iota(jnp.int32, (c, RW_PAIR), 1) < RW_HD

    def stack(x):
        return jnp.concatenate([jnp.where(head0, x, 0.0), jnp.where(head0, 0.0, x)], axis=0)

    def twice(x):
        return jnp.concatenate([x, x], axis=0)

    n = 2 * c
    ii = lax.broadcasted_iota(jnp.int32, (n, n), 0)
    jj = lax.broadcasted_iota(jnp.int32, (n, n), 1)
    same_head = (ii >> 6) == (jj >> 6)
    strict = same_head & (ii > jj)
    incl = same_head & (ii >= jj)

    ats = [stack(blk(at, *q)) for q in probs]
    rts = [stack(blk(rt, *q)) for q in probs]
    vs = [stack(blk(v, *q)) for q in probs]
    aa = [_mm(jnp.concatenate([a_, r_], axis=0),
              jnp.concatenate([twice(blk(bh, *q)), twice(blk(kh, *q))], axis=0), NT, CHUNK_PASSES)
          for a_, r_, q in zip(ats, rts, probs)]
    yield MXU_STAGE
    dab = [jnp.where(strict, x[0:n, 0:n], 0.0) for x in aa]
    dak = [jnp.where(strict, x[0:n, n:2 * n], 0.0) for x in aa]
    drbk = [jnp.concatenate([jnp.where(incl, x[n:2 * n, 0:n], 0.0),
                             jnp.where(incl, x[n:2 * n, n:2 * n], 0.0)], axis=1) for x in aa]
    dv = [_mm(d, v_, NN, CHUNK_PASSES) for d, v_ in zip(dak, vs)]
    tinv = yield from _neumann_inverse(dab, INV_PASSES)
    gh = [_mm(t_, jnp.concatenate([a_, d], axis=1), NN, INV_PASSES) for t_, a_, d in zip(tinv, ats, dv)]
    yield MXU_STAGE
    zeros = jnp.zeros((n, n), F32)
    qy = [_mm(d, jnp.concatenate([x, jnp.concatenate([zeros, v_], axis=1)], axis=0), NN, CHUNK_PASSES)
          for d, x, v_ in zip(drbk, gh, vs)]
    yield MXU_STAGE
    qe = [r_ + x[:, 0:n] for r_, x in zip(rts, qy)]
    y0 = [x[:, n:2 * n] for x in qy]
    bbs = [stack(blk(bb, *q)) for q in probs]
    kbs = [stack(blk(kb, *q)) for q in probs]
    mz = [_mm(jnp.concatenate([x, jnp.concatenate([zeros, v_], axis=1)], axis=0),
              jnp.concatenate([b_, k_], axis=0), TN, STATE_PASSES)
          for x, v_, b_, k_ in zip(gh, vs, bbs, kbs)]
    yield MXU_STAGE
    eye = ii == jj
    mx = [jnp.where(eye, blk(gam, *q)[0:1, :], 0.0) + x[0:n] for x, q in zip(mz, probs)]
    zz = [x[n:2 * n] for x in mz]

    assert order[0] == r0, "sub-tiles must reach their state update in time order"
    state = [s_s[p] for p in range(N_PAIRS)]
    y_rows = []
    for ci in range(tt // c):
        y_pairs = []
        for p in range(N_PAIRS):
            i = ci * N_PAIRS + p
            ys = _mm(qe[i], state[p], NT, STATE_PASSES) + y0[i]
            y_pairs.append(ys[0:c] + ys[c:n])
            state[p] = _mm(state[p], mx[i], NN, STATE_PASSES) + zz[i]
        y_rows.append(jnp.concatenate(y_pairs, axis=1))
    for p in range(N_PAIRS):
        s_s[p] = state[p]
    order[0] = r0 + tt
    yield MXU_STAGE
    y = jnp.concatenate(y_rows, axis=0)

    inv_hd = 1.0 / RW_HD
    d = y - segsum(y) * inv_hd
    yield VPU_STAGE
    yn = d * lax.rsqrt(segsum(d * d) * inv_hd + LNX_EPS) * lnw_ref[...] + lnb_ref[...]
    yn = yn + segsum(r * k2 * rk_ref[...]) * v
    y_ref[0, r0:r0 + tt, :] = (yn * gate).astype(y_ref.dtype)


def _rwkv_kernel(zr_ref, shift_ref, s0_ref, mu_ref, w0_ref, a0_ref, w2a2_ref, g2_ref, kk_ref, ka_ref,
                 rk_ref, lnw_ref, lnb_ref, seg_ref, y_ref, sfin_ref, s_s, prev_s, *, tt, sub, t_valid):
    t = pl.program_id(1)

    @pl.when(t == 0)
    def _():
        s_s[...] = s0_ref[0]
        prev_s[0:1, :] = shift_ref[0]

    order = [0]
    pending = [_rwkv_rows(zr_ref, mu_ref, w0_ref, a0_ref, w2a2_ref, g2_ref, kk_ref, ka_ref, rk_ref, lnw_ref,
                          lnb_ref, seg_ref, y_ref, s_s, prev_s, r0, sub, t_valid, order)
               for r0 in range(0, tt, sub)]
    active = []
    while pending or active:
        if pending and all(stage == MXU_STAGE for _, stage in active):
            active.append([pending.pop(0), VPU_STAGE])
        for item in list(active):
            try:
                item[1] = next(item[0])
            except StopIteration:
                active.remove(item)
    prev_s[0:1, :] = zr_ref[0, tt - 1:tt, :]

    @pl.when(t == pl.num_programs(1) - 1)
    def _():
        sfin_ref[0] = s_s[...]


def _rwkv(zr, shift0, s0_pairs, lp, t_valid):
    b, tp, _ = zr.shape
    tt = _row_tile(tp, RW_TILE)
    assert t_valid == tp or tp == tt, "padding is only supported inside a single time tile"
    vec = lambda w: _const_spec((1, w))
    return pl.pallas_call(
        functools.partial(_rwkv_kernel, tt=tt, sub=min(tt, RW_SUB), t_valid=tt if t_valid == tp else t_valid),
        grid=(b, tp // tt),
        in_specs=[
            pl.BlockSpec((1, tt, RW_COLS), lambda i, t: (i, t, 0)),
            pl.BlockSpec((1, 1, RW_COLS), lambda i, t: (i, 0, 0)),
            pl.BlockSpec((1, N_PAIRS, RW_PAIR, RW_PAIR), lambda i, t: (i, 0, 0, 0)),
            vec(RW_COLS), vec(RW_W), vec(RW_W),
            _const_spec((DECAY_RANK + AAA_RANK, 2 * RW_W)), _const_spec((GATE_RANK, RW_W)),
            vec(RW_W), vec(RW_W), vec(RW_W), vec(RW_W), vec(RW_W),
            _const_spec((RW_W, RW_W)),
        ],
        out_specs=[
            pl.BlockSpec((1, tt, RW_W), lambda i, t: (i, t, 0)),
            pl.BlockSpec((1, N_PAIRS, RW_PAIR, RW_PAIR), lambda i, t: (i, 0, 0, 0)),
        ],
        out_shape=[
            jax.ShapeDtypeStruct((b, tp, RW_W), BF16),
            jax.ShapeDtypeStruct((b, N_PAIRS, RW_PAIR, RW_PAIR), F32),
        ],
        scratch_shapes=[
            pltpu.VMEM((N_PAIRS, RW_PAIR, RW_PAIR), F32),
            pltpu.VMEM((8, RW_COLS), F32),
        ],
        compiler_params=pltpu.CompilerParams(
            dimension_semantics=("parallel", "arbitrary"), vmem_limit_bytes=VMEM_LIMIT_ROWS),
        name="rwkv",
    )(zr, shift0, s0_pairs, lp["mu"], lp["w0"], lp["a0"], lp["w2a2"], lp["g2"], lp["k_k"], lp["k_a"],
      lp["r_k"], lp["lnx_w"], lp["lnx_b"], lp["seg"])


def _to_pairs(s):
    b = s.shape[0]
    s = s.reshape(b, N_PAIRS, 2, RW_HD, RW_HD)
    eye = jnp.eye(2, dtype=s.dtype)
    return jnp.einsum("bphvk,hg->bphvgk", s, eye).reshape(b, N_PAIRS, RW_PAIR, RW_PAIR)


def _from_pairs(sp):
    b = sp.shape[0]
    sp = sp.reshape(b, N_PAIRS, 2, RW_HD, 2, RW_HD)
    return jnp.stack([sp[:, :, 0, :, 0, :], sp[:, :, 1, :, 1, :]], axis=2).reshape(b, RW_HEADS, RW_HD, RW_HD)


def _stack_maps(q):
    lane = lax.broadcasted_iota(jnp.int32, q.shape, 1)
    zero = jnp.zeros_like(q)
    return jnp.concatenate([jnp.where(lane < DF_HD, q, zero), jnp.where(lane < DF_HD, zero, q)], axis=0)


def _attn_prompt_kernel(sc_ref, q_ref, k_ref, kpos_ref, vt_ref, g_ref, dbias_ref, o_ref, m_s, acc_s, s_s, smax_s,
                        *, tb):
    h = pl.program_id(1)
    i = pl.program_id(2)
    lam, out_scale, slope = sc_ref[0], sc_ref[1], sc_ref[2 + h]
    qt = q_ref[0].astype(F32).T
    dim = lax.broadcasted_iota(jnp.int32, (DF_VD, tb), 0)
    q_pos = jnp.where(dim == 0, slope * POS_SPLIT, jnp.where(dim == 1, slope, 0.0))
    qs_t = jnp.concatenate(
        [jnp.concatenate([jnp.where(dim < DF_HD, qt, 0.0), jnp.where(dim < DF_HD, 0.0, qt)], axis=1),
         jnp.concatenate([q_pos, q_pos], axis=1)], axis=0).astype(BF16)

    def scores(j):
        keys = jnp.concatenate([k_ref[0, j], kpos_ref[j]], axis=1)
        return jnp.dot(keys, qs_t, preferred_element_type=F32)

    ones = jnp.ones((ONES_ROWS, tb), BF16)

    def weighted_values(j, p):
        return jnp.dot(jnp.concatenate([vt_ref[0, j], ones], axis=0), p, preferred_element_type=F32)

    def stash_scores(j):
        sc = scores(j)
        s_s[...] = sc
        smax_s[...] = jnp.max(sc, axis=0, keepdims=True)

    s = scores(i) + dbias_ref[0]
    stash_scores(0)
    m = jnp.max(s, axis=0, keepdims=True)
    m_s[...] = m
    acc_s[...] = weighted_values(i, jnp.exp(s - m).astype(BF16))

    def step(j, stash_next):
        s = s_s[...]
        m_old = m_s[...]
        m_new = jnp.maximum(m_old, smax_s[...])
        if stash_next:
            stash_scores(j + 1)
        p = jnp.exp(s - m_new).astype(BF16)
        acc_s[...] = jnp.exp(m_old - m_new) * acc_s[...] + weighted_values(j, p)
        m_s[...] = m_new

    n_lead = jnp.maximum(i - 1, 0)
    n_trips = n_lead // ATTN_TRIP

    def trip(jj, carry):
        for u in range(ATTN_TRIP):
            step(ATTN_TRIP * jj + u, True)
        return carry

    lax.fori_loop(0, n_trips, trip, 0)
    for u in range(ATTN_TRIP - 1):
        @pl.when(n_lead - ATTN_TRIP * n_trips > u)
        def _():
            step(ATTN_TRIP * n_trips + u, True)

    @pl.when(i > 0)
    def _():
        step(i - 1, False)

    acc = acc_s[...]
    on = acc[0:DF_VD] / acc[DF_VD:DF_VD + 1]
    ot = on[:, 0:tb] - lam * on[:, tb:2 * tb]
    ot = ot * lax.rsqrt(jnp.mean(ot * ot, axis=0, keepdims=True) + SUBLN_EPS)
    o_ref[0] = (ot.T * g_ref[...] * out_scale).astype(o_ref.dtype)


def _attn_prompt(scal, q, k, vt, g, tb):
    b, t, _ = q.shape
    nb = t // tb
    pos = jnp.arange(t, dtype=jnp.int32).reshape(nb, tb, 1)
    lane = jnp.arange(V7X_LANES, dtype=jnp.int32)
    kpos = jnp.where(lane == 0, pos // POS_SPLIT, jnp.where(lane == 1, pos % POS_SPLIT, 0)).astype(BF16)
    kl = jnp.arange(tb, dtype=jnp.int32)[:, None]
    ql = jnp.arange(2 * tb, dtype=jnp.int32)[None, :] % tb
    after = jnp.maximum(kl - ql, 0).astype(F32)
    dbias = jnp.where((kl // CHUNK) <= (ql // CHUNK), -2.0 * scal[2:2 + DF_HEADS, None, None] * after, NEG_BIG)
    blk = lambda: pl.BlockSpec((1, tb, DF_VD), lambda bi, h, i: (bi, i, h))
    return pl.pallas_call(
        functools.partial(_attn_prompt_kernel, tb=tb),
        grid=(b, DF_HEADS, nb),
        in_specs=[pl.BlockSpec(memory_space=pltpu.SMEM), blk(),
                  pl.BlockSpec((1, nb, tb, DF_VD), lambda bi, h, i: (bi, 0, 0, h)),
                  _const_spec((nb, tb, V7X_LANES)),
                  pl.BlockSpec((1, nb, DF_VD, tb), lambda bi, h, i: (bi, 0, h, 0)),
                  _const_spec((1, DF_VD)),
                  pl.BlockSpec((1, tb, 2 * tb), lambda bi, h, i: (h, 0, 0))],
        out_specs=blk(),
        out_shape=jax.ShapeDtypeStruct((b, t, DF_V), BF16),
        scratch_shapes=[pltpu.VMEM((1, 2 * tb), F32),
                        pltpu.VMEM((DF_VD + ONES_ROWS, 2 * tb), F32),
                        pltpu.VMEM((tb, 2 * tb), F32),
                        pltpu.VMEM((1, 2 * tb), F32)],
        compiler_params=pltpu.CompilerParams(
            dimension_semantics=("parallel", "parallel", "arbitrary"), vmem_limit_bytes=VMEM_LIMIT_ATTN),
        name="attn_prompt",
    )(scal, q, k, kpos, vt, g, dbias)


def _attn_sample_kernel(sc_ref, q_ref, kn_ref, vn_ref, kp_ref, vp_ref, g_ref, o_ref, *, t, past):
    lam, out_scale = sc_ref[0], sc_ref[1]
    for h in range(DF_HEADS):
        slope = sc_ref[2 + h]
        lanes = slice(h * DF_VD, (h + 1) * DF_VD)
        qs = _stack_maps(q_ref[0, :, lanes])

        def scores(keys, k0):
            nk = keys.shape[0]
            s = lax.dot_general(qs, keys.astype(BF16), NT, preferred_element_type=F32)
            qpos = past + lax.rem(lax.broadcasted_iota(jnp.int32, (2 * t, nk), 0), t)
            kpos = k0 + lax.broadcasted_iota(jnp.int32, (2 * t, nk), 1)
            s = s - slope * jnp.abs(qpos - kpos).astype(F32)
            return jnp.where((kpos >> 6) <= (qpos >> 6), s, NEG_BIG)

        sp = scores(kp_ref[0, :, h, :], 0)
        sn = scores(kn_ref[0, :, lanes], past)
        m = jnp.maximum(jnp.max(sp, axis=-1, keepdims=True), jnp.max(sn, axis=-1, keepdims=True))
        pp = jnp.exp(sp - m)
        pn = jnp.exp(sn - m)
        l = jnp.sum(pp, axis=-1, keepdims=True) + jnp.sum(pn, axis=-1, keepdims=True)
        acc = (jnp.dot(pp.astype(BF16), vp_ref[0, :, h, :].astype(BF16), preferred_element_type=F32)
               + jnp.dot(pn.astype(BF16), vn_ref[0, :, lanes].astype(BF16), preferred_element_type=F32))
        on = acc / l
        o = on[0:t] - lam * on[t:2 * t]
        o_ref[0, :, lanes] = (_rms(o, g_ref[...], SUBLN_EPS) * out_scale).astype(o_ref.dtype)


def _attn_sample(scal, q, k, v, cache_k, cache_v, layer, g):
    b, t, _ = q.shape
    past = cache_k.shape[2]
    new = lambda: pl.BlockSpec((1, t, DF_V), lambda bi: (bi, 0, 0))
    old = lambda: pl.BlockSpec((None, 1, past, DF_HEADS, DF_VD), lambda bi: (layer, bi, 0, 0, 0))
    return pl.pallas_call(
        functools.partial(_attn_sample_kernel, t=t, past=past),
        grid=(b,),
        in_specs=[pl.BlockSpec(memory_space=pltpu.SMEM), new(), new(), new(), old(), old(),
                  _const_spec((1, DF_VD))],
        out_specs=new(),
        out_shape=jax.ShapeDtypeStruct((b, t, DF_V), BF16),
        compiler_params=pltpu.CompilerParams(dimension_semantics=("parallel",), vmem_limit_bytes=VMEM_LIMIT_ROWS),
        name="attn_sample",
    )(scal, q, k, v, cache_k, cache_v, g)


FF_STEP = 256


def _mlp_kernel(x_ref, y_ref, o_ref, wo_ref, g_ref, wg_ref, wu_ref, wd_ref, gf_ref, out_ref, h_s, *, final_norm):
    x1 = (x_ref[...]
          + jnp.dot(y_ref[...], wo_ref[0:RW_W, :], preferred_element_type=F32)
          + jnp.dot(o_ref[...], wo_ref[RW_W:RW_W + DF_V, :], preferred_element_type=F32))
    out_ref[...] = x1
    h_s[...] = _rms(x1, g_ref[...], RMS_EPS).astype(BF16)
    for c in range(0, D_FF, FF_STEP):
        h2 = h_s[...]
        gt = jnp.dot(h2, wg_ref[:, c:c + FF_STEP], preferred_element_type=F32)
        up = jnp.dot(h2, wu_ref[:, c:c + FF_STEP], preferred_element_type=F32)
        act = (gt * jax.nn.sigmoid(gt) * up).astype(BF16)
        out_ref[...] += jnp.dot(act, wd_ref[c:c + FF_STEP, :], preferred_element_type=F32)
    if final_norm:
        out_ref[...] = _rms(out_ref[...], gf_ref[...], RMS_EPS)


def _mlp(x2d, y_rw, o_df, wo, g, wg, wu, wd, final_g, final_norm):
    n = x2d.shape[0]
    tm = _row_tile(n, 512)
    row = lambda w: pl.BlockSpec((tm, w), lambda i: (i, 0))
    return pl.pallas_call(
        functools.partial(_mlp_kernel, final_norm=final_norm),
        grid=(n // tm,),
        in_specs=[row(D_MODEL), row(RW_W), row(DF_V), _const_spec((RW_W + DF_V, D_MODEL)),
                  _const_spec((1, D_MODEL)), _const_spec((D_MODEL, D_FF)), _const_spec((D_MODEL, D_FF)),
                  _const_spec((D_FF, D_MODEL)), _const_spec((1, D_MODEL))],
        out_specs=row(D_MODEL),
        out_shape=jax.ShapeDtypeStruct((n, D_MODEL), F32),
        scratch_shapes=[pltpu.VMEM((tm, D_MODEL), BF16)],
        compiler_params=_vmem_limit(VMEM_LIMIT_MLP),
        name="mlp",
    )(x2d, y_rw, o_df, wo, g.reshape(1, D_MODEL), wg, wu, wd, final_g.reshape(1, D_MODEL))


def _layer(x, shift0, s0_pairs, attend, lp, vt_block, layer, depth, kv_all):
    b, t, _ = x.shape
    n = b * t
    zr, q, k_all, v_all, kb, vb = _inproj(x.reshape(n, D_MODEL), lp["norm1"], lp["w_in"], vt_block, layer, depth,
                                          kv_all)
    zr = zr.reshape(b, t, RW_COLS)
    tp = -(-t // RW_CHUNK) * RW_CHUNK
    zr_p = zr if tp == t else jnp.pad(zr, ((0, 0), (0, tp - t), (0, 0)))
    y_rw, s_fin = _rwkv(zr_p, shift0, s0_pairs, lp, t)
    if tp != t:
        y_rw = y_rw[:, :t]
    o = attend(q.reshape(b, t, DF_QK), kb, vb)
    x_new = _mlp(x.reshape(n, D_MODEL), y_rw.reshape(n, RW_W), o.reshape(n, DF_V), lp["w_out"], lp["norm2"],
                 lp["wg"], lp["wu"], lp["wd"], lp["final_g"], layer == depth - 1)
    return x_new.reshape(b, t, D_MODEL), zr[:, t - 1:t], s_fin, (k_all, v_all)


@jax.jit
def kernel(x_prompt, x_sample, cache_k, cache_v, state_wkv, state_shift, norm1_g, w_in, rw_mu, rw_w0, rw_w2,
           rw_a0, rw_a2, rw_g2, rw_k_k, rw_k_a, rw_r_k, rw_lnx_w, rw_lnx_b, df_lq1, df_lk1, df_lq2, df_lk2,
           df_subln_g, w_out, norm2_g, ffn_w_gate, ffn_w_up, ffn_w_down, final_g):
    depth = w_in.shape[0]
    bp, tp = x_prompt.shape[0], x_prompt.shape[1]
    bs, ts = x_sample.shape[0], x_sample.shape[1]
    past = cache_k.shape[2]
    tb = _row_tile(tp, ATTN_BLOCK)
    assert tb & (tb - 1) == 0 and tb % CHUNK == 0
    slopes = 2.0 ** (-8.0 * jnp.arange(1, DF_HEADS + 1, dtype=F32) / DF_HEADS)
    head = jnp.arange(RW_W) // RW_HD
    seg = (head[:, None] == head[None, :]).astype(BF16)
    zero_blk = jnp.zeros((DECAY_RANK, RW_W), F32)
    w_in_b, w_out_b = w_in.astype(BF16), w_out.astype(BF16)
    wg_b, wu_b, wd_b = ffn_w_gate.astype(BF16), ffn_w_up.astype(BF16), ffn_w_down.astype(BF16)

    xp, xs = x_prompt, x_sample
    kv_buffers = lambda n: tuple(jnp.zeros((depth, n, DF_HEADS, DF_VD), F32) for _ in range(2))
    kv_p, kv_s = kv_buffers(bp * tp), kv_buffers(bs * ts)
    outs = [[] for _ in range(4)]
    for l in range(depth):
        lam_init = 0.8 - 0.6 * math.exp(-0.3 * l)
        lam = (jnp.exp(jnp.sum((df_lq1[l] * df_lk1[l]).astype(F32)))
               - jnp.exp(jnp.sum((df_lq2[l] * df_lk2[l]).astype(F32))) + lam_init)
        scal = jnp.concatenate([jnp.stack([lam, jnp.asarray(1.0 - lam_init, F32)]), slopes]).astype(F32)
        row = lambda u: u[l].reshape(1, -1)
        lp = dict(
            norm1=norm1_g[l], w_in=w_in_b[l], mu=row(rw_mu), w0=row(rw_w0), a0=row(rw_a0),
            w2a2=jnp.concatenate([jnp.concatenate([rw_w2[l], zero_blk], axis=1),
                                  jnp.concatenate([zero_blk, rw_a2[l]], axis=1)], axis=0).astype(BF16),
            g2=rw_g2[l].astype(BF16), k_k=row(rw_k_k), k_a=row(rw_k_a), r_k=row(rw_r_k),
            lnx_w=row(rw_lnx_w), lnx_b=row(rw_lnx_b), seg=seg, w_out=w_out_b[l], norm2=norm2_g[l],
            wg=wg_b[l], wu=wu_b[l], wd=wd_b[l], final_g=final_g)
        g_sub = df_subln_g[l].reshape(1, DF_VD)

        att_p = lambda q, kb, vt: _attn_prompt(scal, q, kb.reshape(bp, tp // tb, tb, DF_QK),
                                               vt.reshape(bp, tp // tb, DF_V, tb), g_sub, tb)
        xp, shp, sp, kv_p = _layer(xp, jnp.zeros((bp, 1, RW_COLS), F32),
                                   jnp.zeros((bp, N_PAIRS, RW_PAIR, RW_PAIR), F32), att_p, lp, tb, l, depth, kv_p)
        att_s = lambda q, kb, vb: _attn_sample(scal, q, kb.reshape(bs, ts, DF_QK), vb.reshape(bs, ts, DF_V),
                                               cache_k, cache_v, l, g_sub)
        xs, shs, ss, kv_s = _layer(xs, state_shift[l], _to_pairs(state_wkv[l]), att_s, lp, None, l, depth, kv_s)
        for lst, val in zip(outs, (_from_pairs(sp), shp, _from_pairs(ss), shs)):
            lst.append(val)

    wkv_p, shift_p, wkv_s, shift_s = (jnp.stack(lst) for lst in outs)
    per_token = lambda u, b, t: u.reshape(depth, b, t, DF_HEADS, DF_VD)
    return (xp, xs, per_token(kv_p[0], bp, tp), per_token(kv_p[1], bp, tp), wkv_p, shift_p,
            per_token(kv_s[0], bs, ts), per_token(kv_s[1], bs, ts), wkv_s, shift_s)
```

```python
import functools
import math

import jax
import jax.numpy as jnp
from jax import lax
from jax.experimental import pallas as pl
from jax.experimental.pallas import tpu as pltpu

F32 = jnp.float32
BF16 = jnp.bfloat16

D_MODEL = 1024
CHUNK = 64
RW_HEADS = 8
RW_HD = 64
RW_W = RW_HEADS * RW_HD
DECAY_RANK = 64
AAA_RANK = 64
GATE_RANK = 128
RW_COLS = 3 * RW_W + DECAY_RANK + AAA_RANK + GATE_RANK
DF_HEADS = 4
DF_HD = 64
DF_VD = 2 * DF_HD
DF_QK = DF_HEADS * 2 * DF_HD
DF_V = DF_HEADS * DF_VD
P_IN = RW_COLS + 2 * DF_QK + DF_V
D_FF = 2816
RMS_EPS = 1e-6
LNX_EPS = 64e-5
SUBLN_EPS = 1e-5
QK_SCALE = DF_HD ** -0.5

V7X_LANES = 128
V7X_MXU_COLS = 256
V7X_VMEM_BYTES = 64 << 20
VMEM_LIMIT_ROWS = (V7X_VMEM_BYTES * 3) // 4
VMEM_LIMIT_MLP = (V7X_VMEM_BYTES * 13) // 16
VMEM_LIMIT_ATTN = V7X_VMEM_BYTES // 2
ATTN_BLOCK = 512
POS_SPLIT = 256
ONES_ROWS = 16
ATTN_TRIP = 4
RW_CHUNK = 64
RW_TILE = 1024
RW_SUB = 128
RW_PAIR = 2 * RW_HD
N_PAIRS = RW_HEADS // 2
NEG_BIG = -1e30

INV_PASSES = 1
CHUNK_PASSES = 1
STATE_PASSES = 1
SEG_PASSES = 1
CUM_PIECES = 2

VPU_STAGE, MXU_STAGE = "vpu", "mxu"

NN = (((1,), (0,)), ((), ()))
NT = (((1,), (1,)), ((), ()))
TN = (((0,), (0,)), ((), ()))


def _vmem_limit(nbytes):
    return pltpu.CompilerParams(vmem_limit_bytes=int(nbytes))


def _row_tile(n, target):
    t = min(n, target)
    while n % t:
        t -= 8
    return t


def _const_spec(shape):
    zeros = (0,) * len(shape)
    return pl.BlockSpec(shape, lambda *_: zeros, pipeline_mode=pl.Buffered(1))


def _split(x, parts):
    out, rem = [], x
    for i in range(parts):
        hi = rem.astype(BF16)
        out.append(hi)
        if i + 1 < parts:
            rem = rem - hi.astype(F32)
    return out


def _mm(a, b, dims=NN, passes=1):
    dot = functools.partial(lax.dot_general, dimension_numbers=dims, preferred_element_type=F32)
    if passes == 1:
        return dot(a.astype(BF16), b.astype(BF16))
    ah, al = _split(a, 2)
    if passes == 2:
        bh = b.astype(BF16)
        return dot(ah, bh) + dot(al, bh)
    bh, bl = _split(b, 2)
    return dot(ah, bh) + dot(al, bh) + dot(ah, bl)


def _rms(x, g, eps):
    return x * lax.rsqrt(jnp.mean(x * x, axis=-1, keepdims=True) + eps) * g


def _inproj_kernel(x_ref, g_ref, w_ref, k_all_ref, v_all_ref, zr_ref, q_ref, k_ref, v_ref, kb_ref, vb_ref, *, vt_block):
    del k_all_ref, v_all_ref
    h = _rms(x_ref[...], g_ref[...], RMS_EPS).astype(BF16)
    step = V7X_MXU_COLS

    def proj(c0):
        return jnp.dot(h, w_ref[:, c0:c0 + step], preferred_element_type=F32)

    for c in range(0, RW_COLS, step):
        zr_ref[:, c:c + step] = proj(c)
    for c in range(0, DF_QK, step):
        q_ref[:, c:c + step] = (proj(RW_COLS + c) * QK_SCALE).astype(q_ref.dtype)
        kc = proj(RW_COLS + DF_QK + c)
        for hh in range(step // DF_VD):
            k_ref[:, c // DF_VD + hh, :] = kc[:, hh * DF_VD:(hh + 1) * DF_VD]
        kb_ref[:, c:c + step] = kc.astype(kb_ref.dtype)
        vc = proj(RW_COLS + 2 * DF_QK + c)
        for hh in range(step // DF_VD):
            v_ref[:, c // DF_VD + hh, :] = vc[:, hh * DF_VD:(hh + 1) * DF_VD]
        if vt_block is None:
            vb_ref[:, c:c + step] = vc.astype(vb_ref.dtype)
        else:
            for rb in range(vb_ref.shape[0]):
                vb_ref[rb, c:c + step, :] = vc[rb * vt_block:(rb + 1) * vt_block, :].T.astype(vb_ref.dtype)


def _inproj(x2d, g, w_bf, vt_block, layer, depth, kv_all):
    n = x2d.shape[0]
    tm = _row_tile(n, 512)
    row = lambda w: pl.BlockSpec((tm, w), lambda i: (i, 0))
    heads = lambda: pl.BlockSpec((None, tm, DF_HEADS, DF_VD), lambda i: (layer, i, 0, 0))
    if vt_block is None:
        vb_spec, vb_shape = row(DF_V), (n, DF_V)
    else:
        assert tm % vt_block == 0
        vb_spec = pl.BlockSpec((tm // vt_block, DF_V, vt_block), lambda i: (i, 0, 0))
        vb_shape = (n // vt_block, DF_V, vt_block)
    return pl.pallas_call(
        functools.partial(_inproj_kernel, vt_block=vt_block),
        grid=(n // tm,),
        in_specs=[row(D_MODEL), _const_spec((1, D_MODEL)), _const_spec((D_MODEL, P_IN))]
        + [pl.BlockSpec(memory_space=pl.ANY), pl.BlockSpec(memory_space=pl.ANY)],
        out_specs=[row(RW_COLS), row(DF_QK), heads(), heads(), row(DF_QK), vb_spec],
        out_shape=[
            jax.ShapeDtypeStruct((n, RW_COLS), F32),
            jax.ShapeDtypeStruct((n, DF_QK), BF16),
            jax.ShapeDtypeStruct((depth, n, DF_HEADS, 2 * DF_HD), F32),
            jax.ShapeDtypeStruct((depth, n, DF_HEADS, DF_VD), F32),
            jax.ShapeDtypeStruct((n, DF_QK), BF16),
            jax.ShapeDtypeStruct(vb_shape, BF16),
        ],
        input_output_aliases={3: 2, 4: 3},
        compiler_params=_vmem_limit(VMEM_LIMIT_ROWS),
        name="inproj",
    )(x2d, g.reshape(1, D_MODEL), w_bf, *kv_all)


def _neumann_inverse(dab, passes):
    n = dab[0].shape[0]
    ii = lax.broadcasted_iota(jnp.int32, (n, n), 0)
    jj = lax.broadcasted_iota(jnp.int32, (n, n), 1)
    same16 = (ii >> 4) == (jj >> 4)
    same32 = (ii >> 5) == (jj >> 5)
    eye = jnp.where(ii == jj, 1.0, 0.0)
    mm = functools.partial(_mm, passes=passes)
    same8 = (ii >> 3) == (jj >> 3)
    x = [jnp.where(same8, d, 0.0) for d in dab]
    p = [eye + xi for xi in x]
    x = [mm(xi, xi) for xi in x]
    yield MXU_STAGE
    for _ in range(1):
        xp = [mm(xi, jnp.concatenate([xi, pi], axis=1)) for xi, pi in zip(x, p)]
        p = [pi + r[:, n:] for pi, r in zip(p, xp)]
        x = [r[:, :n] for r in xp]
        yield MXU_STAGE
    p = [pi + mm(xi, pi) for xi, pi in zip(x, p)]
    yield MXU_STAGE
    for off_mask in (same16 & jnp.logical_not(same8), same32 & jnp.logical_not(same16), jnp.logical_not(same32)):
        off = [jnp.where(off_mask, d, 0.0) for d in dab]
        po = [mm(pi, oi) for pi, oi in zip(p, off)]
        yield MXU_STAGE
        p = [pi + mm(qi, pi) for pi, qi in zip(p, po)]
        yield MXU_STAGE
    return p


def _rwkv_rows(zr_ref, mu_ref, w0_ref, a0_ref, w2a2_ref, g2_ref, kk_ref, ka_ref, rk_ref, lnw_ref, lnb_ref, seg_ref,
               y_ref, s_s, prev_s, r0, tt, t_valid, order):
    zr = zr_ref[0, r0:r0 + tt, :]
    row = lax.broadcasted_iota(jnp.int32, (tt, 1), 0)
    before = prev_s[0:1, :] if r0 == 0 else zr_ref[0, r0 - 1:r0, :]
    zprev = jnp.where(row == 0, before, pltpu.roll(zr, 1, axis=0))
    zm = zr + (zprev - zr) * mu_ref[...]
    yield VPU_STAGE
    r = zm[:, 0:RW_W]
    k = zm[:, RW_W:2 * RW_W]
    v = zm[:, 2 * RW_W:3 * RW_W]
    xwa = zm[:, 3 * RW_W:3 * RW_W + DECAY_RANK + AAA_RANK]
    gd = zm[:, 3 * RW_W + DECAY_RANK + AAA_RANK:RW_COLS]
    lane = lax.broadcasted_iota(jnp.int32, xwa.shape, 1)
    xwa = jnp.where(lane < DECAY_RANK, jnp.tanh(xwa), xwa)
    lora = jnp.dot(xwa.astype(BF16), w2a2_ref[...], preferred_element_type=F32)
    yield VPU_STAGE
    lw = -math.exp(-0.5) * jax.nn.sigmoid(w0_ref[...] + lora[:, 0:RW_W])
    a = jax.nn.sigmoid(a0_ref[...] + lora[:, RW_W:2 * RW_W])
    gate = jnp.dot(jax.nn.sigmoid(gd).astype(BF16), g2_ref[...], preferred_element_type=F32)
    yield VPU_STAGE

    def segsum(x):
        return sum(jnp.dot(part, seg_ref[...], preferred_element_type=F32) for part in _split(x, SEG_PASSES))

    kkn = k * kk_ref[...]
    kkn = kkn * lax.rsqrt(jnp.maximum(segsum(kkn * kkn), 1e-24))
    yield VPU_STAGE
    k2 = k * (1.0 + (a - 1.0) * ka_ref[...])
    if t_valid < tt:
        valid = row < t_valid
        lw = jnp.where(valid, lw, 0.0)
        kkn = jnp.where(valid, kkn, 0.0)
        k2 = jnp.where(valid, k2, 0.0)
        v = jnp.where(valid, v, 0.0)

    c = RW_CHUNK
    ti = lax.broadcasted_iota(jnp.int32, (tt, tt), 0)
    tj = lax.broadcasted_iota(jnp.int32, (tt, tt), 1)
    cum = jnp.where(((ti >> 6) == (tj >> 6)) & (ti >= tj), 1.0, 0.0).astype(BF16)
    g = sum(jnp.dot(cum, part, preferred_element_type=F32) for part in _split(lw, CUM_PIECES))
    gc = jnp.concatenate([jnp.broadcast_to(g[e - 1:e, :], (c, RW_W)) for e in range(c, tt + 1, c)], axis=0)
    yield VPU_STAGE
    ieg = jnp.exp(-g)
    egc = jnp.exp(gc - g)
    b = kkn * a
    at = -kkn * jnp.exp(g - lw)
    rt = r * jnp.exp(g)
    yield VPU_STAGE
    bh, kh = b * ieg, k2 * ieg
    bb, kb = b * egc, k2 * egc
    gam = jnp.exp(gc)
    yield VPU_STAGE

    probs = [(ci, p) for ci in range(tt // c) for p in range(N_PAIRS)]

    def blk(x, ci, p):
        return x[ci * c:(ci + 1) * c, p * RW_PAIR:(p + 1) * RW_PAIR]

    head0 = lax.broadcasted_iota(jnp.int32, (c, RW_PAIR), 1) < RW_HD

    def stack(x):
        return jnp.concatenate([jnp.where(head0, x, 0.0), jnp.where(head0, 0.0, x)], axis=0)

    def twice(x):
        return jnp.concatenate([x, x], axis=0)

    n = 2 * c
    ii = lax.broadcasted_iota(jnp.int32, (n, n), 0)
    jj = lax.broadcasted_iota(jnp.int32, (n, n), 1)
    same_head = (ii >> 6) == (jj >> 6)
    strict = same_head & (ii > jj)
    incl = same_head & (ii >= jj)

    ats = [stack(blk(at, *q)) for q in probs]
    rts = [stack(blk(rt, *q)) for q in probs]
    vs = [stack(blk(v, *q)) for q in probs]
    aa = [_mm(jnp.concatenate([a_, r_], axis=0),
              jnp.concatenate([twice(blk(bh, *q)), twice(blk(kh, *q))], axis=0), NT, CHUNK_PASSES)
          for a_, r_, q in zip(ats, rts, probs)]
    yield MXU_STAGE
    dab = [jnp.where(strict, x[0:n, 0:n], 0.0) for x in aa]
    dak = [jnp.where(strict, x[0:n, n:2 * n], 0.0) for x in aa]
    drbk = [jnp.concatenate([jnp.where(incl, x[n:2 * n, 0:n], 0.0),
                             jnp.where(incl, x[n:2 * n, n:2 * n], 0.0)], axis=1) for x in aa]
    dv = [_mm(d, v_, NN, CHUNK_PASSES) for d, v_ in zip(dak, vs)]
    tinv = yield from _neumann_inverse(dab, INV_PASSES)
    gh = [_mm(t_, jnp.concatenate([a_, d], axis=1), NN, INV_PASSES) for t_, a_, d in zip(tinv, ats, dv)]
    yield MXU_STAGE
    zeros = jnp.zeros((n, n), F32)
    qy = [_mm(d, jnp.concatenate([x, jnp.concatenate([zeros, v_], axis=1)], axis=0), NN, CHUNK_PASSES)
          for d, x, v_ in zip(drbk, gh, vs)]
    yield MXU_STAGE
    qe = [r_ + x[:, 0:n] for r_, x in zip(rts, qy)]
    y0 = [x[:, n:2 * n] for x in qy]
    bbs = [stack(blk(bb, *q)) for q in probs]
    kbs = [stack(blk(kb, *q)) for q in probs]
    mz = [_mm(jnp.concatenate([x, jnp.concatenate([zeros, v_], axis=1)], axis=0),
              jnp.concatenate([b_, k_], axis=0), TN, STATE_PASSES)
          for x, v_, b_, k_ in zip(gh, vs, bbs, kbs)]
    yield MXU_STAGE
    eye = ii == jj
    mx = [jnp.where(eye, blk(gam, *q)[0:1, :], 0.0) + x[0:n] for x, q in zip(mz, probs)]
    zz = [x[n:2 * n] for x in mz]

    assert order[0] == r0, "sub-tiles must reach their state update in time order"
    state = [s_s[p] for p in range(N_PAIRS)]
    y_rows = []
    for ci in range(tt // c):
        y_pairs = []
        for p in range(N_PAIRS):
            i = ci * N_PAIRS + p
            ys = _mm(qe[i], state[p], NT, STATE_PASSES) + y0[i]
            y_pairs.append(ys[0:c] + ys[c:n])
            state[p] = _mm(state[p], mx[i], NN, STATE_PASSES) + zz[i]
        y_rows.append(jnp.concatenate(y_pairs, axis=1))
    for p in range(N_PAIRS):
        s_s[p] = state[p]
    order[0] = r0 + tt
    yield MXU_STAGE
    y = jnp.concatenate(y_rows, axis=0)

    inv_hd = 1.0 / RW_HD
    d = y - segsum(y) * inv_hd
    yield VPU_STAGE
    yn = d * lax.rsqrt(segsum(d * d) * inv_hd + LNX_EPS) * lnw_ref[...] + lnb_ref[...]
    yn = yn + segsum(r * k2 * rk_ref[...]) * v
    y_ref[0, r0:r0 + tt, :] = (yn * gate).astype(y_ref.dtype)


def _rwkv_kernel(zr_ref, shift_ref, s0_ref, mu_ref, w0_ref, a0_ref, w2a2_ref, g2_ref, kk_ref, ka_ref,
                 rk_ref, lnw_ref, lnb_ref, seg_ref, y_ref, sfin_ref, s_s, prev_s, *, tt, sub, t_valid):
    t = pl.program_id(1)

    @pl.when(t == 0)
    def _():
        s_s[...] = s0_ref[0]
        prev_s[0:1, :] = shift_ref[0]

    order = [0]
    pending = [_rwkv_rows(zr_ref, mu_ref, w0_ref, a0_ref, w2a2_ref, g2_ref, kk_ref, ka_ref, rk_ref, lnw_ref,
                          lnb_ref, seg_ref, y_ref, s_s, prev_s, r0, sub, t_valid, order)
               for r0 in range(0, tt, sub)]
    active = []
    while pending or active:
        if pending and all(stage == MXU_STAGE for _, stage in active):
            active.append([pending.pop(0), VPU_STAGE])
        for item in list(active):
            try:
                item[1] = next(item[0])
            except StopIteration:
                active.remove(item)
    prev_s[0:1, :] = zr_ref[0, tt - 1:tt, :]

    @pl.when(t == pl.num_programs(1) - 1)
    def _():
        sfin_ref[0] = s_s[...]


def _rwkv(zr, shift0, s0_pairs, lp, t_valid):
    b, tp, _ = zr.shape
    tt = _row_tile(tp, RW_TILE)
    assert t_valid == tp or tp == tt, "padding is only supported inside a single time tile"
    vec = lambda w: _const_spec((1, w))
    return pl.pallas_call(
        functools.partial(_rwkv_kernel, tt=tt, sub=min(tt, RW_SUB), t_valid=tt if t_valid == tp else t_valid),
        grid=(b, tp // tt),
        in_specs=[
            pl.BlockSpec((1, tt, RW_COLS), lambda i, t: (i, t, 0)),
            pl.BlockSpec((1, 1, RW_COLS), lambda i, t: (i, 0, 0)),
            pl.BlockSpec((1, N_PAIRS, RW_PAIR, RW_PAIR), lambda i, t: (i, 0, 0, 0)),
            vec(RW_COLS), vec(RW_W), vec(RW_W),
            _const_spec((DECAY_RANK + AAA_RANK, 2 * RW_W)), _const_spec((GATE_RANK, RW_W)),
            vec(RW_W), vec(RW_W), vec(RW_W), vec(RW_W), vec(RW_W),
            _const_spec((RW_W, RW_W)),
        ],
        out_specs=[
            pl.BlockSpec((1, tt, RW_W), lambda i, t: (i, t, 0)),
            pl.BlockSpec((1, N_PAIRS, RW_PAIR, RW_PAIR), lambda i, t: (i, 0, 0, 0)),
        ],
        out_shape=[
            jax.ShapeDtypeStruct((b, tp, RW_W), BF16),
            jax.ShapeDtypeStruct((b, N_PAIRS, RW_PAIR, RW_PAIR), F32),
        ],
        scratch_shapes=[
            pltpu.VMEM((N_PAIRS, RW_PAIR, RW_PAIR), F32),
            pltpu.VMEM((8, RW_COLS), F32),
        ],
        compiler_params=pltpu.CompilerParams(
            dimension_semantics=("parallel", "arbitrary"), vmem_limit_bytes=VMEM_LIMIT_ROWS),
        name="rwkv",
    )(zr, shift0, s0_pairs, lp["mu"], lp["w0"], lp["a0"], lp["w2a2"], lp["g2"], lp["k_k"], lp["k_a"],
      lp["r_k"], lp["lnx_w"], lp["lnx_b"], lp["seg"])


def _to_pairs(s):
    b = s.shape[0]
    s = s.reshape(b, N_PAIRS, 2, RW_HD, RW_HD)
    eye = jnp.eye(2, dtype=s.dtype)
    return jnp.einsum("bphvk,hg->bphvgk", s, eye).reshape(b, N_PAIRS, RW_PAIR, RW_PAIR)


def _from_pairs(sp):
    b = sp.shape[0]
    sp = sp.reshape(b, N_PAIRS, 2, RW_HD, 2, RW_HD)
    return jnp.stack([sp[:, :, 0, :, 0, :], sp[:, :, 1, :, 1, :]], axis=2).reshape(b, RW_HEADS, RW_HD, RW_HD)


def _stack_maps(q):
    lane = lax.broadcasted_iota(jnp.int32, q.shape, 1)
    zero = jnp.zeros_like(q)
    return jnp.concatenate([jnp.where(lane < DF_HD, q, zero), jnp.where(lane < DF_HD, zero, q)], axis=0)


def _attn_prompt_kernel(sc_ref, q_ref, k_ref, kpos_ref, vt_ref, g_ref, dbias_ref, o_ref, m_s, acc_s, s_s, smax_s,
                        *, tb):
    h = pl.program_id(1)
    i = pl.program_id(2)
    lam, out_scale, slope = sc_ref[0], sc_ref[1], sc_ref[2 + h]
    qt = q_ref[0].astype(F32).T
    dim = lax.broadcasted_iota(jnp.int32, (DF_VD, tb), 0)
    q_pos = jnp.where(dim == 0, slope * POS_SPLIT, jnp.where(dim == 1, slope, 0.0))
    qs_t = jnp.concatenate(
        [jnp.concatenate([jnp.where(dim < DF_HD, qt, 0.0), jnp.where(dim < DF_HD, 0.0, qt)], axis=1),
         jnp.concatenate([q_pos, q_pos], axis=1)], axis=0).astype(BF16)

    def scores(j):
        keys = jnp.concatenate([k_ref[0, j], kpos_ref[j]], axis=1)
        return jnp.dot(keys, qs_t, preferred_element_type=F32)

    ones = jnp.ones((ONES_ROWS, tb), BF16)

    def weighted_values(j, p):
        return jnp.dot(jnp.concatenate([vt_ref[0, j], ones], axis=0), p, preferred_element_type=F32)

    def stash_scores(j):
        sc = scores(j)
        s_s[...] = sc
        smax_s[...] = jnp.max(sc, axis=0, keepdims=True)

    s = scores(i) + dbias_ref[0]
    stash_scores(0)
    m = jnp.max(s, axis=0, keepdims=True)
    m_s[...] = m
    acc_s[...] = weighted_values(i, jnp.exp(s - m).astype(BF16))

    def step(j, stash_next):
        s = s_s[...]
        m_old = m_s[...]
        m_new = jnp.maximum(m_old, smax_s[...])
        if stash_next:
            stash_scores(j + 1)
        p = jnp.exp(s - m_new).astype(BF16)
        acc_s[...] = jnp.exp(m_old - m_new) * acc_s[...] + weighted_values(j, p)
        m_s[...] = m_new

    n_lead = jnp.maximum(i - 1, 0)
    n_trips = n_lead // ATTN_TRIP

    def trip(jj, carry):
        for u in range(ATTN_TRIP):
            step(ATTN_TRIP * jj + u, True)
        return carry

    lax.fori_loop(0, n_trips, trip, 0)
    for u in range(ATTN_TRIP - 1):
        @pl.when(n_lead - ATTN_TRIP * n_trips > u)
        def _():
            step(ATTN_TRIP * n_trips + u, True)

    @pl.when(i > 0)
    def _():
        step(i - 1, False)

    acc = acc_s[...]
    on = acc[0:DF_VD] / acc[DF_VD:DF_VD + 1]
    ot = on[:, 0:tb] - lam * on[:, tb:2 * tb]
    ot = ot * lax.rsqrt(jnp.mean(ot * ot, axis=0, keepdims=True) + SUBLN_EPS)
    o_ref[0] = (ot.T * g_ref[...] * out_scale).astype(o_ref.dtype)


def _attn_prompt(scal, q, k, vt, g, tb):
    b, t, _ = q.shape
    nb = t // tb
    pos = jnp.arange(t, dtype=jnp.int32).reshape(nb, tb, 1)
    lane = jnp.arange(V7X_LANES, dtype=jnp.int32)
    kpos = jnp.where(lane == 0, pos // POS_SPLIT, jnp.where(lane == 1, pos % POS_SPLIT, 0)).astype(BF16)
    kl = jnp.arange(tb, dtype=jnp.int32)[:, None]
    ql = jnp.arange(2 * tb, dtype=jnp.int32)[None, :] % tb
    after = jnp.maximum(kl - ql, 0).astype(F32)
    dbias = jnp.where((kl // CHUNK) <= (ql // CHUNK), -2.0 * scal[2:2 + DF_HEADS, None, None] * after, NEG_BIG)
    blk = lambda: pl.BlockSpec((1, tb, DF_VD), lambda bi, h, i: (bi, i, h))
    return pl.pallas_call(
        functools.partial(_attn_prompt_kernel, tb=tb),
        grid=(b, DF_HEADS, nb),
        in_specs=[pl.BlockSpec(memory_space=pltpu.SMEM), blk(),
                  pl.BlockSpec((1, nb, tb, DF_VD), lambda bi, h, i: (bi, 0, 0, h)),
                  _const_spec((nb, tb, V7X_LANES)),
                  pl.BlockSpec((1, nb, DF_VD, tb), lambda bi, h, i: (bi, 0, h, 0)),
                  _const_spec((1, DF_VD)),
                  pl.BlockSpec((1, tb, 2 * tb), lambda bi, h, i: (h, 0, 0))],
        out_specs=blk(),
        out_shape=jax.ShapeDtypeStruct((b, t, DF_V), BF16),
        scratch_shapes=[pltpu.VMEM((1, 2 * tb), F32),
                        pltpu.VMEM((DF_VD + ONES_ROWS, 2 * tb), F32),
                        pltpu.VMEM((tb, 2 * tb), F32),
                        pltpu.VMEM((1, 2 * tb), F32)],
        compiler_params=pltpu.CompilerParams(
            dimension_semantics=("parallel", "parallel", "arbitrary"), vmem_limit_bytes=VMEM_LIMIT_ATTN),
        name="attn_prompt",
    )(scal, q, k, kpos, vt, g, dbias)


def _attn_sample_kernel(sc_ref, q_ref, kn_ref, vn_ref, kp_ref, vp_ref, g_ref, o_ref, *, t, past):
    lam, out_scale = sc_ref[0], sc_ref[1]
    for h in range(DF_HEADS):
        slope = sc_ref[2 + h]
        lanes = slice(h * DF_VD, (h + 1) * DF_VD)
        qs = _stack_maps(q_ref[0, :, lanes])

        def scores(keys, k0):
            nk = keys.shape[0]
            s = lax.dot_general(qs, keys.astype(BF16), NT, preferred_element_type=F32)
            qpos = past + lax.rem(lax.broadcasted_iota(jnp.int32, (2 * t, nk), 0), t)
            kpos = k0 + lax.broadcasted_iota(jnp.int32, (2 * t, nk), 1)
            s = s - slope * jnp.abs(qpos - kpos).astype(F32)
            return jnp.where((kpos >> 6) <= (qpos >> 6), s, NEG_BIG)

        sp = scores(kp_ref[0, :, h, :], 0)
        sn = scores(kn_ref[0, :, lanes], past)
        m = jnp.maximum(jnp.max(sp, axis=-1, keepdims=True), jnp.max(sn, axis=-1, keepdims=True))
        pp = jnp.exp(sp - m)
        pn = jnp.exp(sn - m)
        l = jnp.sum(pp, axis=-1, keepdims=True) + jnp.sum(pn, axis=-1, keepdims=True)
        acc = (jnp.dot(pp.astype(BF16), vp_ref[0, :, h, :].astype(BF16), preferred_element_type=F32)
               + jnp.dot(pn.astype(BF16), vn_ref[0, :, lanes].astype(BF16), preferred_element_type=F32))
        on = acc / l
        o = on[0:t] - lam * on[t:2 * t]
        o_ref[0, :, lanes] = (_rms(o, g_ref[...], SUBLN_EPS) * out_scale).astype(o_ref.dtype)


def _attn_sample(scal, q, k, v, cache_k, cache_v, layer, g):
    b, t, _ = q.shape
    past = cache_k.shape[2]
    new = lambda: pl.BlockSpec((1, t, DF_V), lambda bi: (bi, 0, 0))
    old = lambda: pl.BlockSpec((None, 1, past, DF_HEADS, DF_VD), lambda bi: (layer, bi, 0, 0, 0))
    return pl.pallas_call(
        functools.partial(_attn_sample_kernel, t=t, past=past),
        grid=(b,),
        in_specs=[pl.BlockSpec(memory_space=pltpu.SMEM), new(), new(), new(), old(), old(),
                  _const_spec((1, DF_VD))],
        out_specs=new(),
        out_shape=jax.ShapeDtypeStruct((b, t, DF_V), BF16),
        compiler_params=pltpu.CompilerParams(dimension_semantics=("parallel",), vmem_limit_bytes=VMEM_LIMIT_ROWS),
        name="attn_sample",
    )(scal, q, k, v, cache_k, cache_v, g)


FF_STEP = 256


def _mlp_kernel(x_ref, y_ref, o_ref, wo_ref, g_ref, wg_ref, wu_ref, wd_ref, gf_ref, out_ref, h_s, *, final_norm):
    x1 = (x_ref[...]
          + jnp.dot(y_ref[...], wo_ref[0:RW_W, :], preferred_element_type=F32)
          + jnp.dot(o_ref[...], wo_ref[RW_W:RW_W + DF_V, :], preferred_element_type=F32))
    out_ref[...] = x1
    h_s[...] = _rms(x1, g_ref[...], RMS_EPS).astype(BF16)
    for c in range(0, D_FF, FF_STEP):
        h2 = h_s[...]
        gt = jnp.dot(h2, wg_ref[:, c:c + FF_STEP], preferred_element_type=F32)
        up = jnp.dot(h2, wu_ref[:, c:c + FF_STEP], preferred_element_type=F32)
        act = (gt * jax.nn.sigmoid(gt) * up).astype(BF16)
        out_ref[...] += jnp.dot(act, wd_ref[c:c + FF_STEP, :], preferred_element_type=F32)
    if final_norm:
        out_ref[...] = _rms(out_ref[...], gf_ref[...], RMS_EPS)


def _mlp(x2d, y_rw, o_df, wo, g, wg, wu, wd, final_g, final_norm):
    n = x2d.shape[0]
    tm = _row_tile(n, 512)
    row = lambda w: pl.BlockSpec((tm, w), lambda i: (i, 0))
    return pl.pallas_call(
        functools.partial(_mlp_kernel, final_norm=final_norm),
        grid=(n // tm,),
        in_specs=[row(D_MODEL), row(RW_W), row(DF_V), _const_spec((RW_W + DF_V, D_MODEL)),
                  _const_spec((1, D_MODEL)), _const_spec((D_MODEL, D_FF)), _const_spec((D_MODEL, D_FF)),
                  _const_spec((D_FF, D_MODEL)), _const_spec((1, D_MODEL))],
        out_specs=row(D_MODEL),
        out_shape=jax.ShapeDtypeStruct((n, D_MODEL), F32),
        scratch_shapes=[pltpu.VMEM((tm, D_MODEL), BF16)],
        compiler_params=_vmem_limit(VMEM_LIMIT_MLP),
        name="mlp",
    )(x2d, y_rw, o_df, wo, g.reshape(1, D_MODEL), wg, wu, wd, final_g.reshape(1, D_MODEL))


def _layer(x, shift0, s0_pairs, attend, lp, vt_block, layer, depth, kv_all):
    b, t, _ = x.shape
    n = b * t
    zr, q, k_all, v_all, kb, vb = _inproj(x.reshape(n, D_MODEL), lp["norm1"], lp["w_in"], vt_block, layer, depth,
                                          kv_all)
    zr = zr.reshape(b, t, RW_COLS)
    tp = -(-t // RW_CHUNK) * RW_CHUNK
    zr_p = zr if tp == t else jnp.pad(zr, ((0, 0), (0, tp - t), (0, 0)))
    y_rw, s_fin = _rwkv(zr_p, shift0, s0_pairs, lp, t)
    if tp != t:
        y_rw = y_rw[:, :t]
    o = attend(q.reshape(b, t, DF_QK), kb, vb)
    x_new = _mlp(x.reshape(n, D_MODEL), y_rw.reshape(n, RW_W), o.reshape(n, DF_V), lp["w_out"], lp["norm2"],
                 lp["wg"], lp["wu"], lp["wd"], lp["final_g"], layer == depth - 1)
    return x_new.reshape(b, t, D_MODEL), zr[:, t - 1:t], s_fin, (k_all, v_all)


@jax.jit
def kernel(x_prompt, x_sample, cache_k, cache_v, state_wkv, state_shift, norm1_g, w_in, rw_mu, rw_w0, rw_w2,
           rw_a0, rw_a2, rw_g2, rw_k_k, rw_k_a, rw_r_k, rw_lnx_w, rw_lnx_b, df_lq1, df_lk1, df_lq2, df_lk2,
           df_subln_g, w_out, norm2_g, ffn_w_gate, ffn_w_up, ffn_w_down, final_g):
    depth = w_in.shape[0]
    bp, tp = x_prompt.shape[0], x_prompt.shape[1]
    bs, ts = x_sample.shape[0], x_sample.shape[1]
    past = cache_k.shape[2]
    tb = _row_tile(tp, ATTN_BLOCK)
    assert tb & (tb - 1) == 0 and tb % CHUNK == 0
    slopes = 2.0 ** (-8.0 * jnp.arange(1, DF_HEADS + 1, dtype=F32) / DF_HEADS)
    head = jnp.arange(RW_W) // RW_HD
    seg = (head[:, None] == head[None, :]).astype(BF16)
    zero_blk = jnp.zeros((DECAY_RANK, RW_W), F32)
    w_in_b, w_out_b = w_in.astype(BF16), w_out.astype(BF16)
    wg_b, wu_b, wd_b = ffn_w_gate.astype(BF16), ffn_w_up.astype(BF16), ffn_w_down.astype(BF16)

    xp, xs = x_prompt, x_sample
    kv_buffers = lambda n: tuple(jnp.zeros((depth, n, DF_HEADS, DF_VD), F32) for _ in range(2))
    kv_p, kv_s = kv_buffers(bp * tp), kv_buffers(bs * ts)
    outs = [[] for _ in range(4)]
    for l in range(depth):
        lam_init = 0.8 - 0.6 * math.exp(-0.3 * l)
        lam = (jnp.exp(jnp.sum((df_lq1[l] * df_lk1[l]).astype(F32)))
               - jnp.exp(jnp.sum((df_lq2[l] * df_lk2[l]).astype(F32))) + lam_init)
        scal = jnp.concatenate([jnp.stack([lam, jnp.asarray(1.0 - lam_init, F32)]), slopes]).astype(F32)
        row = lambda u: u[l].reshape(1, -1)
        lp = dict(
            norm1=norm1_g[l], w_in=w_in_b[l], mu=row(rw_mu), w0=row(rw_w0), a0=row(rw_a0),
            w2a2=jnp.concatenate([jnp.concatenate([rw_w2[l], zero_blk], axis=1),
                                  jnp.concatenate([zero_blk, rw_a2[l]], axis=1)], axis=0).astype(BF16),
            g2=rw_g2[l].astype(BF16), k_k=row(rw_k_k), k_a=row(rw_k_a), r_k=row(rw_r_k),
            lnx_w=row(rw_lnx_w), lnx_b=row(rw_lnx_b), seg=seg, w_out=w_out_b[l], norm2=norm2_g[l],
            wg=wg_b[l], wu=wu_b[l], wd=wd_b[l], final_g=final_g)
        g_sub = df_subln_g[l].reshape(1, DF_VD)

        att_p = lambda q, kb, vt: _attn_prompt(scal, q, kb.reshape(bp, tp // tb, tb, DF_QK),
                                               vt.reshape(bp, tp // tb, DF_V, tb), g_sub, tb)
        xp, shp, sp, kv_p = _layer(xp, jnp.zeros((bp, 1, RW_COLS), F32),
                                   jnp.zeros((bp, N_PAIRS, RW_PAIR, RW_PAIR), F32), att_p, lp, tb, l, depth, kv_p)
        att_s = lambda q, kb, vb: _attn_sample(scal, q, kb.reshape(bs, ts, DF_QK), vb.reshape(bs, ts, DF_V),
                                               cache_k, cache_v, l, g_sub)
        xs, shs, ss, kv_s = _layer(xs, state_shift[l], _to_pairs(state_wkv[l]), att_s, lp, None, l, depth, kv_s)
        for lst, val in zip(outs, (_from_pairs(sp), shp, _from_pairs(ss), shs)):
            lst.append(val)

    wkv_p, shift_p, wkv_s, shift_s = (jnp.stack(lst) for lst in outs)
    per_token = lambda u, b, t: u.reshape(depth, b, t, DF_HEADS, DF_VD)
    return (xp, xs, per_token(kv_p[0], bp, tp), per_token(kv_p[1], bp, tp), wkv_p, shift_p,
            per_token(kv_s[0], bs, ts), per_token(kv_s[1], bs, ts), wkv_s, shift_s)
```
